```python
import math
import jax, jax.numpy as jnp
from jax import lax
import numpy as np

D_MODEL = 1024
BATCH = 4
SEQ = 4096
DEPTH = 4
DEC_BATCH = 32
DEC_SEQ = 1
PAST_LEN = 8192
PAGE_SIZE = 128

HEAD_DIM = 64
SB_HEADS = D_MODEL // 4 // HEAD_DIM
FOX_HEADS = D_MODEL // 4 // HEAD_DIM
HG_DK = 128
HG_DV = 128
HG_HEADS = D_MODEL // 2 // HG_DV
SB_W = SB_HEADS * HEAD_DIM
FOX_W = FOX_HEADS * HEAD_DIM
HG_KW = HG_HEADS * HG_DK
HG_VW = HG_HEADS * HG_DV
D_MIX = SB_W + FOX_W + HG_VW
IN_SIZES = (SB_W, SB_W, SB_W, FOX_W, FOX_W, FOX_W, FOX_HEADS, HG_KW, HG_KW, HG_VW, HG_VW)
N_IN = sum(IN_SIZES)
Q_BLOCK = 128
HGRN_CHUNK = 32
D_FF = 2816
N_EXPERTS = 8
TOP_K = 2
D_FF_EXPERT = 1408
N_DENSE = (DEPTH + 1) // 2
N_MOE = DEPTH // 2
EPS = 1e-6
FOX_F_BIAS_INIT = 3.0

kernel_name = "hymba_sb_fox_hgrn2_decoder_step"


def rms_norm(x, g):
    xf = x.astype(jnp.float32)
    y = xf * lax.rsqrt(jnp.mean(xf * xf, axis=-1, keepdims=True) + EPS)
    return (y * g.astype(jnp.float32)).astype(x.dtype)


def heads(a, n):
    return a.reshape(a.shape[:-1] + (n, a.shape[-1] // n))


def split_columns(p):
    idx = np.cumsum(IN_SIZES)[:-1].tolist()
    return jnp.split(p, idx, axis=-1)


def gather_pages(cache_l, page_table):
    g = cache_l[page_table]
    return g.reshape((g.shape[0], g.shape[1] * g.shape[2]) + g.shape[3:])


def sweep_query_blocks(fn, q_args, q_pos):
    B, T = q_args[0].shape[:2]
    blk = math.gcd(T, Q_BLOCK)
    nb = T // blk

    def to_blocks(a):
        return jnp.moveaxis(a.reshape((B, nb, blk) + a.shape[2:]), 1, 0)

    xs = tuple(to_blocks(a) for a in q_args) + (q_pos.reshape(nb, blk),)
    out = lax.map(lambda blk_args: fn(*blk_args), xs)
    return jnp.moveaxis(out, 0, 1).reshape((B, T) + out.shape[3:])


def stick_breaking_attn(q, k, v, q_pos, k_pos):
    z = jnp.einsum('bqhd,bkhd->bhqk', q.astype(jnp.float32), k.astype(jnp.float32)) * (HEAD_DIM ** -0.5)
    before = (k_pos[None, :] < q_pos[:, None])[None, None]
    log_not = jnp.where(before, jax.nn.log_sigmoid(-z), 0.0)
    log_a = jnp.where(before, z + lax.cumsum(log_not, axis=3, reverse=True), -jnp.inf)
    return jnp.einsum('bhqk,bkhd->bqhd', jnp.exp(log_a), v.astype(jnp.float32))


def forgetting_attn(q, k, v, cum_q, cum_k, q_pos, k_pos):
    s = jnp.einsum('bqhd,bkhd->bhqk', q.astype(jnp.float32), k.astype(jnp.float32)) * (HEAD_DIM ** -0.5)
    s = s + jnp.swapaxes(cum_q, 1, 2)[..., :, None] - jnp.swapaxes(cum_k, 1, 2)[..., None, :]
    causal = (k_pos[None, :] <= q_pos[:, None])[None, None]
    p = jax.nn.softmax(jnp.where(causal, s, -jnp.inf), axis=-1)
    return jnp.einsum('bhqk,bkhd->bqhd', p, v.astype(jnp.float32))


def hgrn2_chunked(q, k, v, log_f, s0):
    B, T, H, DK = q.shape
    DV = v.shape[-1]
    c = math.gcd(T, HGRN_CHUNK)
    n = T // c

    def chunks(a):
        return jnp.moveaxis(a.reshape(B, n, c, H, a.shape[-1]), 1, 0)

    incl = jnp.tril(jnp.ones((c, c), dtype=bool))[None, :, :, None, None]

    def step(S, xs):
        q_c, k_c, v_c, lf_c = xs
        G = jnp.cumsum(lf_c, axis=1)
        G_end = G[:, -1]
        o_inter = jnp.einsum('bthk,bhkv->bthv', q_c * jnp.exp(G), S)
        diff = jnp.where(incl, G[:, :, None] - G[:, None, :], -jnp.inf)
        a = jnp.einsum('bthk,bshk,btshk->bhts', q_c, k_c, jnp.exp(diff))
        o_intra = jnp.einsum('bhts,bshv->bthv', a, v_c)
        k_end = k_c * jnp.exp(G_end[:, None] - G)
        S = S * jnp.exp(G_end)[..., None] + jnp.einsum('bshk,bshv->bhkv', k_end, v_c)
        return S, o_inter + o_intra

    S, o = lax.scan(step, s0, (chunks(q), chunks(k), chunks(v), chunks(log_f)))
    return jnp.moveaxis(o, 0, 1).reshape(B, T, H, DV), S


def token_mixer(xn, w_in_l, w_out_l, f_bias_l, q_norm_g_l, k_norm_g_l, sb_g_l, fox_g_l, hg_g_l, lb_l,
                past_sb_k, past_sb_v, past_fox_k, past_fox_v, past_fox_logf, s0):
    f32 = jnp.float32
    B, T, _ = xn.shape
    P = past_sb_k.shape[1]
    sb_q, sb_k, sb_v, fx_q, fx_k, fx_v, fx_f, hg_q, hg_f, hg_i, hg_g = split_columns(xn @ w_in_l)
    k_pos = jnp.arange(P + T)
    q_pos = P + jnp.arange(T)

    sb_q = heads(sb_q, SB_HEADS)
    sb_k = heads(sb_k, SB_HEADS).astype(past_sb_k.dtype)
    sb_v = heads(sb_v, SB_HEADS).astype(past_sb_v.dtype)
    sb_k_all = jnp.concatenate([past_sb_k, sb_k], axis=1)
    sb_v_all = jnp.concatenate([past_sb_v, sb_v], axis=1)
    sb_o = sweep_query_blocks(lambda qb, pb: stick_breaking_attn(qb, sb_k_all, sb_v_all, pb, k_pos),
                              (sb_q,), q_pos)

    fx_q = rms_norm(heads(fx_q, FOX_HEADS), q_norm_g_l)
    fx_k = rms_norm(heads(fx_k, FOX_HEADS), k_norm_g_l).astype(past_fox_k.dtype)
    fx_v = heads(fx_v, FOX_HEADS).astype(past_fox_v.dtype)
    log_fg = jax.nn.log_sigmoid(fx_f.astype(f32) + f_bias_l.astype(f32))
    cum = jnp.cumsum(jnp.concatenate([past_fox_logf.astype(f32), log_fg], axis=1), axis=1)
    fx_k_all = jnp.concatenate([past_fox_k, fx_k], axis=1)
    fx_v_all = jnp.concatenate([past_fox_v, fx_v], axis=1)
    fx_o = sweep_query_blocks(
        lambda qb, cb, pb: forgetting_attn(qb, fx_k_all, fx_v_all, cb, cum, pb, k_pos),
        (fx_q, cum[:, P:]), q_pos)

    hq = jax.nn.silu(heads(hg_q, HG_HEADS).astype(f32))
    lb = lb_l.reshape(HG_HEADS, HG_DK)
    f_gate = lb + (1.0 - lb) * jax.nn.sigmoid(heads(hg_f, HG_HEADS).astype(f32))
    log_f = jnp.log(f_gate)
    hk = 1.0 - f_gate
    hv = heads(hg_i, HG_HEADS).astype(f32)
    ho, s_new = hgrn2_chunked(hq, hk, hv, log_f, s0)
    ho = rms_norm(ho, hg_g_l) * jax.nn.sigmoid(heads(hg_g, HG_HEADS).astype(f32))

    mix = jnp.concatenate([rms_norm(sb_o.reshape(B, T, SB_W), sb_g_l),
                           rms_norm(fx_o.reshape(B, T, FOX_W), fox_g_l),
                           ho.reshape(B, T, HG_VW)], axis=-1).astype(xn.dtype)
    new_state = (sb_k, sb_v, fx_k, fx_v, log_fg.astype(past_fox_logf.dtype), s_new)
    return mix @ w_out_l, new_state


def swiglu(x, wg, wu, wd):
    return (jax.nn.silu(x @ wg) * (x @ wu)) @ wd


def moe_ffn(x, router_w, router_b, wg, wu, wd):
    logits = (x @ router_w).astype(jnp.float32) + router_b.astype(jnp.float32)
    top_val, top_idx = lax.top_k(logits, TOP_K)
    top_w = jax.nn.softmax(top_val, axis=-1)
    gates = jnp.sum(jax.nn.one_hot(top_idx, N_EXPERTS, dtype=jnp.float32) * top_w[..., None], axis=-2)
    y = jnp.zeros(x.shape, jnp.float32)
    for e in range(N_EXPERTS):
        y = y + gates[..., e:e + 1] * swiglu(x, wg[e], wu[e], wd[e]).astype(jnp.float32)
    return y.astype(x.dtype)


def setup_inputs(seed: int = 0) -> dict:
    key = jax.random.key(seed)
    ks = jax.random.split(key, 28)
    f32 = jnp.float32
    n_pages = PAST_LEN // PAGE_SIZE
    used = DEC_BATCH * n_pages
    n_pool = used + max(1, used // 4)

    def nrm(i, shape, scale=1.0):
        return jax.random.normal(ks[i], shape, f32) * scale

    def gain(i, shape):
        return 1.0 + nrm(i, shape, 0.02)

    perm = jax.random.permutation(ks[8], n_pool)
    page_table = perm[:used].reshape(DEC_BATCH, n_pages).astype(jnp.int32)
    return {
        'x_prompt': nrm(0, (BATCH, SEQ, D_MODEL)),
        'x_sample': nrm(1, (DEC_BATCH, DEC_SEQ, D_MODEL)),
        'cache_sb_k': nrm(2, (DEPTH, n_pool, PAGE_SIZE, SB_HEADS, HEAD_DIM)),
        'cache_sb_v': nrm(3, (DEPTH, n_pool, PAGE_SIZE, SB_HEADS, HEAD_DIM)),
        'cache_fox_k': nrm(4, (DEPTH, n_pool, PAGE_SIZE, FOX_HEADS, HEAD_DIM)),
        'cache_fox_v': nrm(5, (DEPTH, n_pool, PAGE_SIZE, FOX_HEADS, HEAD_DIM)),
        'cache_fox_logf': jax.nn.log_sigmoid(FOX_F_BIAS_INIT + nrm(6, (DEPTH, n_pool, PAGE_SIZE, FOX_HEADS))),
        'state_hgrn': nrm(7, (DEPTH, DEC_BATCH, HG_HEADS, HG_DK, HG_DV), 0.5),
        'page_table': page_table,
        'w_in': nrm(9, (DEPTH, D_MODEL, N_IN), D_MODEL ** -0.5),
        'w_out': nrm(10, (DEPTH, D_MIX, D_MODEL), D_MIX ** -0.5),
        'norm_mix_g': gain(11, (DEPTH, D_MODEL)),
        'norm_ffn_g': gain(12, (DEPTH, D_MODEL)),
        'fox_q_norm_g': gain(13, (DEPTH, HEAD_DIM)),
        'fox_k_norm_g': gain(14, (DEPTH, HEAD_DIM)),
        'fox_f_bias': FOX_F_BIAS_INIT + nrm(15, (DEPTH, FOX_HEADS), 0.1),
        'sb_out_g': gain(16, (DEPTH, SB_W)),
        'fox_out_g': gain(17, (DEPTH, FOX_W)),
        'hgrn_out_g': gain(18, (DEPTH, HG_DV)),
        'hgrn_lb_logits': nrm(19, (DEPTH, HG_KW)),
        'ffn_w_gate': nrm(20, (N_DENSE, D_MODEL, D_FF), D_MODEL ** -0.5),
        'ffn_w_up': nrm(21, (N_DENSE, D_MODEL, D_FF), D_MODEL ** -0.5),
        'ffn_w_down': nrm(22, (N_DENSE, D_FF, D_MODEL), D_FF ** -0.5),
        'moe_router_w': nrm(23, (N_MOE, D_MODEL, N_EXPERTS), D_MODEL ** -0.5),
        'moe_router_b': nrm(24, (N_MOE, N_EXPERTS), 0.01),
        'moe_w_gate': nrm(25, (N_MOE, N_EXPERTS, D_MODEL, D_FF_EXPERT), D_MODEL ** -0.5),
        'moe_w_up': nrm(26, (N_MOE, N_EXPERTS, D_MODEL, D_FF_EXPERT), D_MODEL ** -0.5),
        'moe_w_down': nrm(27, (N_MOE, N_EXPERTS, D_FF_EXPERT, D_MODEL), D_FF_EXPERT ** -0.5),
    }


def reference(x_prompt, x_sample, cache_sb_k, cache_sb_v, cache_fox_k, cache_fox_v, cache_fox_logf,
              state_hgrn, page_table, w_in, w_out, norm_mix_g, norm_ffn_g, fox_q_norm_g, fox_k_norm_g,
              fox_f_bias, sb_out_g, fox_out_g, hgrn_out_g, hgrn_lb_logits, ffn_w_gate, ffn_w_up, ffn_w_down,
              moe_router_w, moe_router_b, moe_w_gate, moe_w_up, moe_w_down):
    f32 = jnp.float32
    sm = jax.nn.softmax(hgrn_lb_logits.astype(f32), axis=0)
    lower_bounds = jnp.cumsum(sm, axis=0) - sm[0:1]

    def run_group(x, past_fn):
        outs = []
        for l in range(DEPTH):
            mix, new = token_mixer(rms_norm(x, norm_mix_g[l]), w_in[l], w_out[l], fox_f_bias[l],
                                   fox_q_norm_g[l], fox_k_norm_g[l], sb_out_g[l], fox_out_g[l],
                                   hgrn_out_g[l], lower_bounds[l], *past_fn(l))
            h = x + mix
            hn = rms_norm(h, norm_ffn_g[l])
            if l % 2 == 0:
                ff = swiglu(hn, ffn_w_gate[l // 2], ffn_w_up[l // 2], ffn_w_down[l // 2])
            else:
                ff = moe_ffn(hn, moe_router_w[l // 2], moe_router_b[l // 2], moe_w_gate[l // 2],
                             moe_w_up[l // 2], moe_w_down[l // 2])
            x = h + ff
            outs.append(new)
        stacked = [jnp.stack([o[i] for o in outs]) for i in range(6)]
        return x, stacked

    def prompt_past(l):
        b = x_prompt.shape[0]
        return (jnp.zeros((b, 0, SB_HEADS, HEAD_DIM), cache_sb_k.dtype),
                jnp.zeros((b, 0, SB_HEADS, HEAD_DIM), cache_sb_v.dtype),
                jnp.zeros((b, 0, FOX_HEADS, HEAD_DIM), cache_fox_k.dtype),
                jnp.zeros((b, 0, FOX_HEADS, HEAD_DIM), cache_fox_v.dtype),
                jnp.zeros((b, 0, FOX_HEADS), cache_fox_logf.dtype),
                jnp.zeros((b, HG_HEADS, HG_DK, HG_DV), f32))

    def sample_past(l):
        return (gather_pages(cache_sb_k[l], page_table),
                gather_pages(cache_sb_v[l], page_table),
                gather_pages(cache_fox_k[l], page_table),
                gather_pages(cache_fox_v[l], page_table),
                gather_pages(cache_fox_logf[l], page_table),
                state_hgrn[l].astype(f32))

    y_prompt, (p_sb_k, p_sb_v, p_fox_k, p_fox_v, p_fox_logf, p_hgrn) = run_group(x_prompt, prompt_past)
    y_sample, (s_sb_k, s_sb_v, s_fox_k, s_fox_v, s_fox_logf, s_hgrn) = run_group(x_sample, sample_past)
    p_hgrn = p_hgrn.astype(state_hgrn.dtype)
    s_hgrn = s_hgrn.astype(state_hgrn.dtype)
    return (y_prompt, y_sample, p_sb_k, p_sb_v, p_fox_k, p_fox_v, p_fox_logf, p_hgrn,
            s_sb_k, s_sb_v, s_fox_k, s_fox_v, s_fox_logf, s_hgrn)
```

```python
import functools

import jax
import jax.numpy as jnp
from jax import lax
from jax.experimental import pallas as pl
from jax.experimental.pallas import tpu as pltpu

F32 = jnp.float32
BF16 = jnp.bfloat16
EPS = 1e-6
NEG_BIG = -1e30

HEAD_DIM = 64
N_HEADS = 4
ATT_W = N_HEADS * HEAD_DIM
HG_D = 128
HG_W = N_HEADS * HG_D
LANES = 128
VMEM_LIMIT = 56 * 1024 * 1024

C_SBQ, C_SBK, C_SBV = 0, 256, 512
C_FXQ, C_FXK, C_FXV = 768, 1024, 1280
C_HGQ, C_HGF, C_HGI, C_HGG = 1536, 2048, 2560, 3072
C_FXF = 3584
N_IN_PAD = 3840


def _cparams(*sem):
    return pltpu.CompilerParams(dimension_semantics=sem, vmem_limit_bytes=VMEM_LIMIT)


def _rms(xf, g_row):
    ms = jnp.mean(xf * xf, axis=-1, keepdims=True)
    return xf * lax.rsqrt(ms + EPS) * g_row


def _softplus(z):
    return jnp.maximum(z, 0.0) + jnp.log(1.0 + jnp.exp(-jnp.abs(z)))


def _sigmoid(x):
    return 1.0 / (1.0 + jnp.exp(-x))


def _split2(x):
    hi = x.astype(BF16)
    lo = (x - hi.astype(F32)).astype(BF16)
    return hi, lo


def _split3(x):
    hi = x.astype(BF16)
    r = x - hi.astype(F32)
    mid = r.astype(BF16)
    lo = (r - mid.astype(F32)).astype(BF16)
    return hi, mid, lo


def _dot(a, b):
    return jnp.dot(a, b, preferred_element_type=F32)


def _dot_nt(a, b):
    return lax.dot_general(a, b, (((1,), (1,)), ((), ())), preferred_element_type=F32)


def _stack_split2(x):
    hi, lo = _split2(x)
    return hi, jnp.concatenate([hi, lo], axis=0)


def _dot3(x, w):
    m = x.shape[0]
    xh, x2 = _stack_split2(x)
    wh, wl = _split2(w)
    r = _dot(x2, wh)
    return r[:m] + r[m:] + _dot(xh, wl)


def _dot3_nt(x, w):
    m = x.shape[0]
    xh, x2 = _stack_split2(x)
    wh, wl = _split2(w)
    r = _dot_nt(x2, wh)
    return r[:m] + r[m:] + _dot_nt(xh, wl)


def _dot_ones(x, ones_bf16):
    m = x.shape[0]
    r = _dot(jnp.concatenate(_split3(x), axis=0), ones_bf16)
    return r[:m] + r[m:2 * m] + r[2 * m:]


def _mm(x, w, precise):
    return _dot3(x, w) if precise else _dot(x.astype(BF16), w)


def _tri(n, rel):
    r = lax.broadcasted_iota(jnp.int32, (n, n), 0)
    c = lax.broadcasted_iota(jnp.int32, (n, n), 1)
    return jnp.where(rel(r, c), 1.0, 0.0).astype(BF16)


def _lower_bounds_kernel(x_ref, o_ref):
    x = x_ref[...]
    depth = x.shape[0]
    m = jnp.max(x, axis=0, keepdims=True)
    e = jnp.exp(x - m)
    sm = e / jnp.sum(e, axis=0, keepdims=True)
    run = jnp.zeros_like(sm[0:1])
    rows = []
    for l in range(depth):
        run = run + sm[l:l + 1]
        rows.append(run - sm[0:1])
    o_ref[...] = jnp.concatenate(rows, axis=0)


def lower_bounds(logits):
    return pl.pallas_call(
        _lower_bounds_kernel,
        out_shape=jax.ShapeDtypeStruct(logits.shape, F32),
        name="hgrn_lower_bounds",
    )(logits)


def _in_proj_kernel(x_ref, g_ref, w_ref, o_ref, xn_ref, *, precise):
    @pl.when(pl.program_id(1) == 0)
    def _():
        xn_ref[...] = _rms(x_ref[...], g_ref[...]).astype(xn_ref.dtype)

    o_ref[...] = _mm(xn_ref[...], w_ref[...], precise)


def in_proj(x, g_row, w, *, tm, tn):
    n, d = x.shape
    n_out = w.shape[1]
    precise = w.dtype == F32
    return pl.pallas_call(
        functools.partial(_in_proj_kernel, precise=precise),
        grid=(n // tm, n_out // tn),
        in_specs=[pl.BlockSpec((tm, d), lambda i, j: (i, 0)),
                  pl.BlockSpec((1, d), lambda i, j: (0, 0)),
                  pl.BlockSpec((d, tn), lambda i, j: (0, j))],
        out_specs=pl.BlockSpec((tm, tn), lambda i, j: (i, j)),
        out_shape=jax.ShapeDtypeStruct((n, n_out), F32),
        scratch_shapes=[pltpu.VMEM((tm, d), F32 if precise else BF16)],
        compiler_params=_cparams("parallel", "arbitrary"),
        name="norm_in_proj",
    )(x, g_row, w)


def _seg_mean_sq(x, seg_ref):
    hi, lo = _split2(x * x)
    return (_dot(hi, seg_ref[...]) + _dot(lo, seg_ref[...])) * (1.0 / HEAD_DIM)


def _fox_prep_kernel(q_ref, k_ref, f_ref, gq_ref, gk_ref, fb_ref, seg_ref,
                     kn_ref, lf_ref, qa_ref, ka_ref, carry_ref):
    tb = q_ref.shape[0]

    @pl.when(pl.program_id(1) == 0)
    def _():
        carry_ref[...] = jnp.zeros_like(carry_ref)

    xq = q_ref[...]
    xk = k_ref[...]
    qn = xq * lax.rsqrt(_seg_mean_sq(xq, seg_ref) + EPS) * gq_ref[...]
    kn = xk * lax.rsqrt(_seg_mean_sq(xk, seg_ref) + EPS) * gk_ref[...]
    kn_ref[...] = kn

    fx = f_ref[...] + fb_ref[...]
    lf = jnp.minimum(fx, 0.0) - jnp.log(1.0 + jnp.exp(-jnp.abs(fx)))
    lf_ref[...] = lf[:, :N_HEADS]

    ltri = _tri(tb, lambda r, c: c <= r)
    p0, p1, p2 = _split3(lf)
    cum = _dot(ltri, p0) + _dot(ltri, p1) + _dot(ltri, p2) + carry_ref[...]
    carry_ref[...] = cum[tb - 1:tb, :]

    c0, c1, c2 = (c.astype(F32) for c in _split3(cum))
    lane = lax.broadcasted_iota(jnp.int32, (tb, HEAD_DIM), 1)
    one = jnp.where(lane < 6, 1.0, 0.0)
    qs = qn * (HEAD_DIM ** -0.5)
    q_parts, k_parts = [], []
    for h in range(N_HEADS):
        a0, a1, a2 = c0[:, h:h + 1], c1[:, h:h + 1], c2[:, h:h + 1]
        ext_q = jnp.where(lane == 0, a0, jnp.where(lane == 1, a1, jnp.where(lane == 2, a2, one)))
        ext_k = jnp.where(lane == 3, -a0, jnp.where(lane == 4, -a1, jnp.where(lane == 5, -a2, one)))
        q_parts += [qs[:, h * HEAD_DIM:(h + 1) * HEAD_DIM], ext_q]
        k_parts += [kn[:, h * HEAD_DIM:(h + 1) * HEAD_DIM], ext_k]
    qa_ref[...] = jnp.concatenate(q_parts, axis=1).astype(BF16)
    ka_ref[...] = jnp.concatenate(k_parts, axis=1).astype(BF16)


def fox_prep(p, gq_row, gk_row, fb_row, seg, *, batch, seq, tb):
    n = p.shape[0]
    nb = seq // tb
    row = lambda b, t: b * nb + t
    aw = N_HEADS * LANES
    return pl.pallas_call(
        _fox_prep_kernel,
        grid=(batch, nb),
        in_specs=[pl.BlockSpec((tb, ATT_W), lambda b, t: (row(b, t), C_FXQ // ATT_W)),
                  pl.BlockSpec((tb, ATT_W), lambda b, t: (row(b, t), C_FXK // ATT_W)),
                  pl.BlockSpec((tb, LANES), lambda b, t: (row(b, t), C_FXF // LANES)),
                  pl.BlockSpec((1, ATT_W), lambda b, t: (0, 0)),
                  pl.BlockSpec((1, ATT_W), lambda b, t: (0, 0)),
                  pl.BlockSpec((1, LANES), lambda b, t: (0, 0)),
                  pl.BlockSpec((ATT_W, ATT_W), lambda b, t: (0, 0))],
        out_specs=[pl.BlockSpec((tb, ATT_W), lambda b, t: (row(b, t), 0)),
                   pl.BlockSpec((tb, N_HEADS), lambda b, t: (row(b, t), 0)),
                   pl.BlockSpec((tb, aw), lambda b, t: (row(b, t), 0)),
                   pl.BlockSpec((tb, aw), lambda b, t: (row(b, t), 0))],
        out_shape=[jax.ShapeDtypeStruct((n, ATT_W), F32),
                   jax.ShapeDtypeStruct((n, N_HEADS), F32),
                   jax.ShapeDtypeStruct((n, aw), BF16),
                   jax.ShapeDtypeStruct((n, aw), BF16)],
        scratch_shapes=[pltpu.VMEM((1, LANES), F32)],
        compiler_params=_cparams("parallel", "arbitrary"),
        name="fox_prep",
    )(p, p, p, gq_row, gk_row, fb_row, seg)


def _store_v_transposed(v_ref, vt_ref, tk):
    for j in range(vt_ref.shape[0]):
        vt_ref[j] = v_ref[j * tk:(j + 1) * tk, :].T.astype(BF16)


def _finish_attention(ot, g_ref, o_ref):
    ms = jnp.mean(ot * ot, axis=0, keepdims=True)
    o_ref[...] = (ot * lax.rsqrt(ms + EPS)).T * g_ref[...]


def _sb_attn_kernel(q_ref, k_ref, v_ref, g_ref, o_ref, vt_ref, acc_ref, r_ref, *, tk):
    i = pl.program_id(1)
    tq = q_ref.shape[0]

    @pl.when(i == 0)
    def _():
        _store_v_transposed(v_ref, vt_ref, tk)

    acc_ref[...] = jnp.zeros_like(acc_ref)
    r_ref[...] = jnp.zeros_like(r_ref)
    q = (q_ref[...] * (HEAD_DIM ** -0.5)).astype(BF16)
    ut = _tri(tk, lambda r, c: c >= r)
    key_i = lax.broadcasted_iota(jnp.int32, (tk, tq), 0)
    qry_i = lax.broadcasted_iota(jnp.int32, (tk, tq), 1)
    before = key_i < qry_i

    def block(j, diagonal):
        kb = k_ref[pl.ds(pl.multiple_of(j * tk, tk), tk), :].astype(BF16)
        for h in range(N_HEADS):
            hs = slice(h * HEAD_DIM, (h + 1) * HEAD_DIM)
            zt = _dot_nt(kb[:, hs], q[:, hs])
            sp = _softplus(zt)
            if diagonal:
                sp = jnp.where(before, sp, 0.0)
            hi, lo = _split2(sp)
            cs = _dot(ut, hi) + _dot(ut, lo)
            r_old = r_ref[h:h + 1, :]
            a = jnp.exp(zt - cs - r_old)
            if diagonal:
                a = jnp.where(before, a, 0.0)
            acc_ref[hs, :] += _dot(vt_ref[j, hs, :], a.astype(BF16))
            r_ref[h:h + 1, :] = r_old + cs[0:1, :]

    block(i, True)

    def body(jj, c):
        block(i - 1 - jj, False)
        return c

    lax.fori_loop(0, i, body, 0)
    _finish_attention(acc_ref[...], g_ref, o_ref)


def sb_attention(p, g_row, *, batch, seq, tq):
    n = p.shape[0]
    nq = seq // tq
    kern = functools.partial(_sb_attn_kernel, tk=tq)
    return pl.pallas_call(
        kern,
        grid=(batch, nq),
        in_specs=[pl.BlockSpec((tq, ATT_W), lambda b, i: (b * nq + i, C_SBQ // ATT_W)),
                  pl.BlockSpec((seq, ATT_W), lambda b, i: (b, C_SBK // ATT_W)),
                  pl.BlockSpec((seq, ATT_W), lambda b, i: (b, C_SBV // ATT_W)),
                  pl.BlockSpec((1, ATT_W), lambda b, i: (0, 0))],
        out_specs=pl.BlockSpec((tq, ATT_W), lambda b, i: (b * nq + i, 0)),
        out_shape=jax.ShapeDtypeStruct((n, ATT_W), F32),
        scratch_shapes=[pltpu.VMEM((nq, ATT_W, tq), BF16),
                        pltpu.VMEM((ATT_W, tq), F32),
                        pltpu.VMEM((8, tq), F32)],
        compiler_params=_cparams("parallel", "arbitrary"),
        name="sb_attention",
    )(p, p, p, g_row)


def _fox_attn_kernel(q_ref, k_ref, v_ref, g_ref, o_ref, vt_ref, acc_ref, m_ref, l_ref, *, tk):
    i = pl.program_id(1)
    tq = q_ref.shape[0]

    @pl.when(i == 0)
    def _():
        _store_v_transposed(v_ref, vt_ref, tk)

    acc_ref[...] = jnp.zeros_like(acc_ref)
    m_ref[...] = jnp.full_like(m_ref, NEG_BIG)
    l_ref[...] = jnp.zeros_like(l_ref)
    key_i = lax.broadcasted_iota(jnp.int32, (tk, tq), 0)
    qry_i = lax.broadcasted_iota(jnp.int32, (tk, tq), 1)
    causal = key_i <= qry_i

    def block(j, diagonal):
        row0 = pl.multiple_of(j * tk, tk)
        for h in range(N_HEADS):
            hs = slice(h * HEAD_DIM, (h + 1) * HEAD_DIM)
            ws = slice(h * LANES, (h + 1) * LANES)
            st = _dot_nt(k_ref[pl.ds(row0, tk), ws], q_ref[:, ws])
            if diagonal:
                st = jnp.where(causal, st, NEG_BIG)
            m_old = m_ref[h:h + 1, :]
            m_new = jnp.maximum(m_old, jnp.max(st, axis=0, keepdims=True))
            alpha = jnp.exp(m_old - m_new)
            pt = jnp.exp(st - m_new)
            l_ref[h:h + 1, :] = alpha * l_ref[h:h + 1, :] + jnp.sum(pt, axis=0, keepdims=True)
            acc_ref[hs, :] = alpha * acc_ref[hs, :] + _dot(vt_ref[j, hs, :], pt.astype(BF16))
            m_ref[h:h + 1, :] = m_new

    block(i, True)

    def body(jj, c):
        block(i - 1 - jj, False)
        return c

    lax.fori_loop(0, i, body, 0)
    parts = [acc_ref[h * HEAD_DIM:(h + 1) * HEAD_DIM, :] / l_ref[h:h + 1, :] for h in range(N_HEADS)]
    _finish_attention(jnp.concatenate(parts, axis=0), g_ref, o_ref)


def fox_attention(qa, ka, p, g_row, *, batch, seq, tq):
    n = p.shape[0]
    nq = seq // tq
    aw = N_HEADS * LANES
    kern = functools.partial(_fox_attn_kernel, tk=tq)
    return pl.pallas_call(
        kern,
        grid=(batch, nq),
        in_specs=[pl.BlockSpec((tq, aw), lambda b, i: (b * nq + i, 0)),
                  pl.BlockSpec((seq, aw), lambda b, i: (b, 0)),
                  pl.BlockSpec((seq, ATT_W), lambda b, i: (b, C_FXV // ATT_W)),
                  pl.BlockSpec((1, ATT_W), lambda b, i: (0, 0))],
        out_specs=pl.BlockSpec((tq, ATT_W), lambda b, i: (b * nq + i, 0)),
        out_shape=jax.ShapeDtypeStruct((n, ATT_W), F32),
        scratch_shapes=[pltpu.VMEM((nq, ATT_W, tq), BF16),
                        pltpu.VMEM((ATT_W, tq), F32),
                        pltpu.VMEM((8, tq), F32),
                        pltpu.VMEM((8, tq), F32)],
        compiler_params=_cparams("parallel", "arbitrary"),
        name="fox_attention",
    )(qa, ka, p, g_row)


def _hgrn_gates(hq, hf, lb):
    q = hq * _sigmoid(hq)
    f = lb + (1.0 - lb) * _sigmoid(hf)
    return q, f, 1.0 - f


def _hgrn_kernel(hq_ref, hf_ref, hi_ref, hg_ref, lb_ref, g_ref, o_ref, s_ref, st_ref, *, sub):
    c = hq_ref.shape[0]
    n_sub = c // sub

    @pl.when(pl.program_id(1) == 0)
    def _():
        st_ref[...] = jnp.zeros_like(st_ref)

    ltri = _tri(c, lambda r, cc: cc <= r)
    t_i = lax.broadcasted_iota(jnp.int32, (sub, sub), 0)
    s_i = lax.broadcasted_iota(jnp.int32, (sub, sub), 1)
    outs = []
    for h in range(N_HEADS):
        ws = slice(h * HG_D, (h + 1) * HG_D)
        q, f, k = _hgrn_gates(hq_ref[:, ws], hf_ref[:, ws], lb_ref[:, ws])
        v = hi_ref[:, ws]
        vb = v.astype(BF16)
        p0, p1, p2 = _split3(jnp.log(f))
        g = _dot(ltri, p0) + _dot(ltri, p1) + _dot(ltri, p2)
        st = st_ref[h]
        o_inter = _dot_nt((q * jnp.exp(g)).astype(BF16), st.astype(BF16))
        rows = []
        for blk in range(n_sub):
            rs = slice(blk * sub, (blk + 1) * sub)
            qb, gb, kb = q[rs], g[rs], k[rs]
            d = jnp.zeros((sub, sub), F32)
            for s in range(sub):
                dec = jnp.exp(jnp.minimum(gb - gb[s:s + 1, :], 0.0))
                col = jnp.sum(qb * kb[s:s + 1, :] * dec, axis=-1, keepdims=True)
                d = jnp.where(s_i == s, col, d)
            d = jnp.where(s_i <= t_i, d, 0.0)
            o_b = o_inter[rs] + _dot(d.astype(BF16), vb[rs])
            if blk > 0:
                lo = blk * sub
                g_ref_row = g[lo - 1:lo, :]
                qs = (qb * jnp.exp(gb - g_ref_row)).astype(BF16)
                ks = (k[:lo] * jnp.exp(g_ref_row - g[:lo])).astype(BF16)
                o_b = o_b + _dot(_dot_nt(qs, ks).astype(BF16), vb[:lo])
            rows.append(o_b)
        o = jnp.concatenate(rows, axis=0)
        g_end = g[c - 1:c, :]
        k_end = (k * jnp.exp(g_end - g)).astype(BF16)
        st_ref[h] = st * jnp.exp(g_end) + _dot(v.T.astype(BF16), k_end)
        outs.append(_rms(o, g_ref[...]) * _sigmoid(hg_ref[:, ws]))
    o_ref[...] = jnp.concatenate(outs, axis=1)

    @pl.when(pl.program_id(1) == pl.num_programs(1) - 1)
    def _():
        for h in range(N_HEADS):
            s_ref[0, h] = st_ref[h].T


def hgrn_prompt(p, lb_row, g_row, *, batch, seq, chunk, sub):
    n = p.shape[0]
    nc = seq // chunk
    col = lambda c0: (lambda b, t: (b * nc + t, c0 // HG_W))
    kern = functools.partial(_hgrn_kernel, sub=sub)
    return pl.pallas_call(
        kern,
        grid=(batch, nc),
        in_specs=[pl.BlockSpec((chunk, HG_W), col(C_HGQ)),
                  pl.BlockSpec((chunk, HG_W), col(C_HGF)),
                  pl.BlockSpec((chunk, HG_W), col(C_HGI)),
                  pl.BlockSpec((chunk, HG_W), col(C_HGG)),
                  pl.BlockSpec((1, HG_W), lambda b, t: (0, 0)),
                  pl.BlockSpec((1, HG_D), lambda b, t: (0, 0))],
        out_specs=[pl.BlockSpec((chunk, HG_W), lambda b, t: (b * nc + t, 0)),
                   pl.BlockSpec((1, N_HEADS, HG_D, HG_D), lambda b, t: (b, 0, 0, 0))],
        out_shape=[jax.ShapeDtypeStruct((n, HG_W), F32),
                   jax.ShapeDtypeStruct((batch, N_HEADS, HG_D, HG_D), F32)],
        scratch_shapes=[pltpu.VMEM((N_HEADS, HG_D, HG_D), F32)],
        compiler_params=_cparams("parallel", "arbitrary"),
        name="hgrn_prompt",
    )(p, p, p, p, lb_row, g_row)


def _out_proj_kernel(x_ref, a_ref, b_ref, c_ref, w_ref, o_ref, *, precise):
    mix = jnp.concatenate([a_ref[...], b_ref[...], c_ref[...]], axis=1)
    o_ref[...] = x_ref[...] + _mm(mix, w_ref[...], precise)


def out_proj(x, sb_o, fx_o, hg_o, w, *, tm):
    n, d = x.shape
    row = lambda i: (i, 0)
    return pl.pallas_call(
        functools.partial(_out_proj_kernel, precise=w.dtype == F32),
        grid=(n // tm,),
        in_specs=[pl.BlockSpec((tm, d), row),
                  pl.BlockSpec((tm, ATT_W), row),
                  pl.BlockSpec((tm, ATT_W), row),
                  pl.BlockSpec((tm, HG_W), row),
                  pl.BlockSpec(w.shape, lambda i: (0, 0))],
        out_specs=pl.BlockSpec((tm, d), row),
        out_shape=jax.ShapeDtypeStruct((n, d), F32),
        compiler_params=_cparams("parallel"),
        name="out_proj",
    )(x, sb_o, fx_o, hg_o, w)


def _ffn_kernel(h_ref, g_ref, wg_ref, wu_ref, wd_ref, o_ref, hn_ref, acc_ref, *, precise):
    j = pl.program_id(1)

    @pl.when(j == 0)
    def _():
        hn_ref[...] = _rms(h_ref[...], g_ref[...]).astype(hn_ref.dtype)
        acc_ref[...] = jnp.zeros_like(acc_ref)

    hn = hn_ref[...]
    gate = _mm(hn, wg_ref[...], precise)
    up = _mm(hn, wu_ref[...], precise)
    acc_ref[...] += _mm(gate * _sigmoid(gate) * up, wd_ref[...], precise)

    @pl.when(j == pl.num_programs(1) - 1)
    def _():
        o_ref[...] = h_ref[...] + acc_ref[...]


def ffn_dense(h, g_row, wg, wu, wd, *, tm, tf):
    n, d = h.shape
    ff = wg.shape[1]
    precise = wg.dtype == F32
    return pl.pallas_call(
        functools.partial(_ffn_kernel, precise=precise),
        grid=(n // tm, ff // tf),
        in_specs=[pl.BlockSpec((tm, d), lambda i, j: (i, 0)),
                  pl.BlockSpec((1, d), lambda i, j: (0, 0)),
                  pl.BlockSpec((d, tf), lambda i, j: (0, j)),
                  pl.BlockSpec((d, tf), lambda i, j: (0, j)),
                  pl.BlockSpec((tf, d), lambda i, j: (j, 0))],
        out_specs=pl.BlockSpec((tm, d), lambda i, j: (i, 0)),
        out_shape=jax.ShapeDtypeStruct((n, d), F32),
        scratch_shapes=[pltpu.VMEM((tm, d), F32 if precise else BF16), pltpu.VMEM((tm, d), F32)],
        compiler_params=_cparams("parallel", "arbitrary"),
        name="ffn_dense",
    )(h, g_row, wg, wu, wd)


def _router_kernel(h_ref, g_ref, w_ref, b_ref, o_ref, *, n_experts):
    hn = _rms(h_ref[...], g_ref[...])
    xh, xl = _split2(hn)
    wh, wl = _split2(w_ref[...])
    logits = _dot(xh, wh) + _dot(xl, wh) + _dot(xh, wl) + b_ref[...]
    lane = lax.broadcasted_iota(jnp.int32, logits.shape, 1).astype(F32)
    logits = jnp.where(lane < n_experts, logits, NEG_BIG)
    m1 = jnp.max(logits, axis=1, keepdims=True)
    i1 = jnp.min(jnp.where(logits == m1, lane, float(LANES)), axis=1, keepdims=True)
    first = lane == i1
    rest = jnp.where(first, NEG_BIG, logits)
    m2 = jnp.max(rest, axis=1, keepdims=True)
    i2 = jnp.min(jnp.where(rest == m2, lane, float(LANES)), axis=1, keepdims=True)
    second = lane == i2
    e = jnp.exp(m2 - m1)
    w1 = 1.0 / (1.0 + e)
    o_ref[...] = jnp.where(first, w1, 0.0) + jnp.where(second, e * w1, 0.0)


def moe_router(h, g_row, w_pad, b_pad, *, tm, n_experts):
    n, d = h.shape
    kern = functools.partial(_router_kernel, n_experts=n_experts)
    return pl.pallas_call(
        kern,
        grid=(n // tm,),
        in_specs=[pl.BlockSpec((tm, d), lambda i: (i, 0)),
                  pl.BlockSpec((1, d), lambda i: (0, 0)),
                  pl.BlockSpec((d, LANES), lambda i: (0, 0)),
                  pl.BlockSpec((1, LANES), lambda i: (0, 0))],
        out_specs=pl.BlockSpec((tm, LANES), lambda i: (i, 0)),
        out_shape=jax.ShapeDtypeStruct((n, LANES), F32),
        compiler_params=_cparams("parallel"),
        name="moe_router",
    )(h, g_row, w_pad, b_pad)


def _moe_kernel(h_ref, g_ref, gates_ref, wg_ref, wu_ref, wd_ref, o_ref, hn_ref, acc_ref, *, precise):
    e = pl.program_id(1)

    @pl.when(e == 0)
    def _():
        hn_ref[...] = _rms(h_ref[...], g_ref[...]).astype(hn_ref.dtype)
        acc_ref[...] = jnp.zeros_like(acc_ref)

    gates = gates_ref[...]
    lane = lax.broadcasted_iota(jnp.int32, gates.shape, 1)
    ge = jnp.sum(jnp.where(lane == e, gates, 0.0), axis=1, keepdims=True)
    hn = hn_ref[...]
    gate = _mm(hn, wg_ref[0], precise)
    up = _mm(hn, wu_ref[0], precise)
    acc_ref[...] += _mm(gate * _sigmoid(gate) * up * ge, wd_ref[0], precise)

    @pl.when(e == pl.num_programs(1) - 1)
    def _():
        o_ref[...] = h_ref[...] + acc_ref[...]


def moe_ffn(h, g_row, gates, wg, wu, wd, *, tm):
    n, d = h.shape
    n_e, _, ffe = wg.shape
    precise = wg.dtype == F32
    return pl.pallas_call(
        functools.partial(_moe_kernel, precise=precise),
        grid=(n // tm, n_e),
        in_specs=[pl.BlockSpec((tm, d), lambda i, e: (i, 0)),
                  pl.BlockSpec((1, d), lambda i, e: (0, 0)),
                  pl.BlockSpec((tm, LANES), lambda i, e: (i, 0)),
                  pl.BlockSpec((1, d, ffe), lambda i, e: (e, 0, 0)),
                  pl.BlockSpec((1, d, ffe), lambda i, e: (e, 0, 0)),
                  pl.BlockSpec((1, ffe, d), lambda i, e: (e, 0, 0))],
        out_specs=pl.BlockSpec((tm, d), lambda i, e: (i, 0)),
        out_shape=jax.ShapeDtypeStruct((n, d), F32),
        scratch_shapes=[pltpu.VMEM((tm, d), F32 if precise else BF16), pltpu.VMEM((tm, d), F32)],
        compiler_params=_cparams("parallel", "arbitrary"),
        name="moe_ffn",
    )(h, g_row, gates, wg, wu, wd)


def _head_rows(width):
    r = lax.broadcasted_iota(jnp.int32, (8, width), 0)
    c = lax.broadcasted_iota(jnp.int32, (8, width), 1)
    return (c // HEAD_DIM) == r


def _lane_to_rows(row_vec):
    r = lax.broadcasted_iota(jnp.int32, (8, LANES), 0)
    c = lax.broadcasted_iota(jnp.int32, (8, LANES), 1)
    return jnp.sum(jnp.where(r == c, row_vec, 0.0), axis=1, keepdims=True)


def _finish_decode(acc, own, g_ref, o_ref):
    o = jnp.sum(jnp.where(own, acc, 0.0), axis=0, keepdims=True)
    o_ref[0] = _rms(o, g_ref[...])


def _sb_decode_kernel(pt_ref, q_ref, g_ref, *rest, pps):
    k_refs, v_refs = rest[:pps], rest[pps:2 * pps]
    o_ref, acc_ref, r_ref = rest[2 * pps:]
    c = pl.program_id(1)

    @pl.when(c == 0)
    def _():
        acc_ref[...] = jnp.zeros_like(acc_ref)
        r_ref[...] = jnp.zeros_like(r_ref)

    own = _head_rows(ATT_W)
    qb = jnp.where(own, q_ref[0] * (HEAD_DIM ** -0.5), 0.0)
    u = _tri(LANES, lambda r, cc: r >= cc)
    order = list(reversed(range(pps)))
    zs = [_dot3_nt(qb, k_refs[i][0]) for i in order]
    css = [_dot_ones(_softplus(z), u) for z in zs]
    acc = acc_ref[...]
    run = r_ref[...]
    for i, z, cs in zip(order, zs, css):
        acc = acc + _dot3(jnp.exp(z - cs - run), v_refs[i][0])
        run = run + cs[:, 0:1]
    acc_ref[...] = acc
    r_ref[...] = run

    @pl.when(c == pl.num_programs(1) - 1)
    def _():
        _finish_decode(acc, own, g_ref, o_ref)


def _page_specs(n_chunks, pps, page_off, block):
    def spec(i):
        def index(b, c, pt):
            return (page_off + pt[b, (n_chunks - 1 - c) * pps + i],) + (0,) * (len(block) - 1)
        return pl.BlockSpec(block, index)
    return [spec(i) for i in range(pps)]


def sb_decode(q3, g_row, cache_k, cache_v, page_table, *, page_off, pps):
    nb = q3.shape[0]
    n_pages = page_table.shape[1]
    n_chunks = n_pages // pps
    page = cache_k.shape[1]
    kern = functools.partial(_sb_decode_kernel, pps=pps)
    pages = _page_specs(n_chunks, pps, page_off, (1, page, ATT_W))
    grid_spec = pltpu.PrefetchScalarGridSpec(
        num_scalar_prefetch=1,
        grid=(nb, n_chunks),
        in_specs=[pl.BlockSpec((1, 1, ATT_W), lambda b, c, pt: (b, 0, 0)),
                  pl.BlockSpec((1, ATT_W), lambda b, c, pt: (0, 0))] + pages + pages,
        out_specs=pl.BlockSpec((1, 1, ATT_W), lambda b, c, pt: (b, 0, 0)),
        scratch_shapes=[pltpu.VMEM((8, ATT_W), F32), pltpu.VMEM((8, 1), F32)],
    )
    return pl.pallas_call(
        kern,
        grid_spec=grid_spec,
        out_shape=jax.ShapeDtypeStruct((nb, 1, ATT_W), F32),
        compiler_params=_cparams("parallel", "arbitrary"),
        name="sb_decode",
    )(page_table, q3, g_row, *([cache_k] * pps), *([cache_v] * pps))


def _fox_decode_kernel(pt_ref, q_ref, k_ref, v_ref, f_ref, gq_ref, gk_ref, fb_ref, g_ref, *rest, pps):
    k_refs, v_refs, f_refs = rest[:pps], rest[pps:2 * pps], rest[2 * pps:3 * pps]
    o_ref, kn_ref, lf_ref, acc_ref, m_ref, l_ref, r_ref, qb_ref = rest[3 * pps:]
    c = pl.program_id(1)
    own = _head_rows(ATT_W)

    @pl.when(c == 0)
    def _():
        def head_norm(x_row, gain_row):
            xb = jnp.where(own, x_row, 0.0)
            ms = jnp.sum(xb * xb, axis=1, keepdims=True) * (1.0 / HEAD_DIM)
            return xb * lax.rsqrt(ms + EPS) * gain_row

        qn = head_norm(q_ref[0], gq_ref[...]) * (HEAD_DIM ** -0.5)
        kn = head_norm(k_ref[0], gk_ref[...])
        kn_ref[0] = jnp.sum(kn, axis=0, keepdims=True)
        fx = f_ref[0] + fb_ref[...]
        lf = jnp.minimum(fx, 0.0) - jnp.log(1.0 + jnp.exp(-jnp.abs(fx)))
        lf_ref[0] = lf
        qb_ref[...] = qn
        m_ref[...] = jnp.sum(qn * kn, axis=1, keepdims=True)
        l_ref[...] = jnp.ones_like(l_ref)
        acc_ref[...] = jnp.broadcast_to(v_ref[0], acc_ref.shape)
        r_ref[...] = _lane_to_rows(lf)

    qb = qb_ref[...]
    u = _tri(LANES, lambda r, cc: r >= cc)
    m_old, run = m_ref[...], r_ref[...]
    pad = jnp.zeros((8 - N_HEADS, LANES), F32)
    order = list(reversed(range(pps)))
    zs = [_dot3_nt(qb, k_refs[i][0]) for i in order]
    lfs = [jnp.concatenate([f_refs[i][0], pad], axis=0) for i in order]
    css = [_dot_ones(lf, u) for lf in lfs]
    ss = []
    for z, lf, cs in zip(zs, lfs, css):
        ss.append(z + (cs - lf) + run)
        run = run + cs[:, 0:1]
    m_new = m_old
    for s in ss:
        m_new = jnp.maximum(m_new, jnp.max(s, axis=1, keepdims=True))
    alpha = jnp.exp(m_old - m_new)
    l = alpha * l_ref[...]
    acc = alpha * acc_ref[...]
    for i, s in zip(order, ss):
        pr = jnp.exp(s - m_new)
        l = l + jnp.sum(pr, axis=1, keepdims=True)
        acc = acc + _dot3(pr, v_refs[i][0])
    acc_ref[...], m_ref[...], l_ref[...], r_ref[...] = acc, m_new, l, run

    @pl.when(c == pl.num_programs(1) - 1)
    def _():
        _finish_decode(acc / l, own, g_ref, o_ref)


def fox_decode(q3, k3, v3, f3, gq_row, gk_row, fb_row, g_row, cache_k, cache_v, cache_ft, page_table,
               *, page_off, pps):
    nb = q3.shape[0]
    n_pages = page_table.shape[1]
    n_chunks = n_pages // pps
    page = cache_k.shape[1]
    kern = functools.partial(_fox_decode_kernel, pps=pps)
    pages = _page_specs(n_chunks, pps, page_off, (1, page, ATT_W))
    f_pages = _page_specs(n_chunks, pps, page_off, (1, N_HEADS, page))
    tok = lambda w: pl.BlockSpec((1, 1, w), lambda b, c, pt: (b, 0, 0))
    par = lambda w: pl.BlockSpec((1, w), lambda b, c, pt: (0, 0))
    grid_spec = pltpu.PrefetchScalarGridSpec(
        num_scalar_prefetch=1,
        grid=(nb, n_chunks),
        in_specs=[tok(ATT_W), tok(ATT_W), tok(ATT_W), tok(LANES),
                  par(ATT_W), par(ATT_W), par(LANES), par(ATT_W)] + pages + pages + f_pages,
        out_specs=[tok(ATT_W), tok(ATT_W), tok(LANES)],
        scratch_shapes=[pltpu.VMEM((8, ATT_W), F32), pltpu.VMEM((8, 1), F32), pltpu.VMEM((8, 1), F32),
                        pltpu.VMEM((8, 1), F32), pltpu.VMEM((8, ATT_W), F32)],
    )
    return pl.pallas_call(
        kern,
        grid_spec=grid_spec,
        out_shape=[jax.ShapeDtypeStruct((nb, 1, ATT_W), F32),
                   jax.ShapeDtypeStruct((nb, 1, ATT_W), F32),
                   jax.ShapeDtypeStruct((nb, 1, LANES), F32)],
        compiler_params=_cparams("parallel", "arbitrary"),
        name="fox_decode",
    )(page_table, q3, k3, v3, f3, gq_row, gk_row, fb_row, g_row,
      *([cache_k] * pps), *([cache_v] * pps), *([cache_ft] * pps))


def _row_to_col(row_vec):
    r = lax.broadcasted_iota(jnp.int32, (HG_D, HG_D), 0)
    c = lax.broadcasted_iota(jnp.int32, (HG_D, HG_D), 1)
    return jnp.sum(jnp.where(r == c, row_vec, 0.0), axis=1, keepdims=True)


def _hgrn_decode_kernel(hq_ref, hf_ref, hi_ref, hg_ref, lb_ref, g_ref, s_ref, o_ref, sn_ref):
    outs = []
    for h in range(N_HEADS):
        ws = slice(h * HG_D, (h + 1) * HG_D)
        q, f, k = _hgrn_gates(hq_ref[0][:, ws], hf_ref[0][:, ws], lb_ref[:, ws])
        s_new = s_ref[0, h] * _row_to_col(f) + _row_to_col(k) * hi_ref[0][:, ws]
        sn_ref[0, h] = s_new
        o = jnp.sum(_row_to_col(q) * s_new, axis=0, keepdims=True)
        outs.append(_rms(o, g_ref[...]) * _sigmoid(hg_ref[0][:, ws]))
    o_ref[0] = jnp.concatenate(outs, axis=1)


def hgrn_decode(hq3, hf3, hi3, hg3, lb_row, g_row, state):
    nb = hq3.shape[0]
    tok = pl.BlockSpec((1, 1, HG_W), lambda b: (b, 0, 0))
    st = pl.BlockSpec((1, N_HEADS, HG_D, HG_D), lambda b: (b, 0, 0, 0))
    return pl.pallas_call(
        _hgrn_decode_kernel,
        grid=(nb,),
        in_specs=[tok, tok, tok, tok,
                  pl.BlockSpec((1, HG_W), lambda b: (0, 0)),
                  pl.BlockSpec((1, HG_D), lambda b: (0, 0)), st],
        out_specs=[tok, st],
        out_shape=[jax.ShapeDtypeStruct((nb, 1, HG_W), F32),
                   jax.ShapeDtypeStruct(state.shape, F32)],
        compiler_params=_cparams("parallel"),
        name="hgrn_decode",
    )(hq3, hf3, hi3, hg3, lb_row, g_row, state)


def _arrange_w_in(w_in_l):
    d = w_in_l.shape[0]
    a = 2 * 3 * ATT_W
    main = jnp.concatenate([w_in_l[:, :a], w_in_l[:, a + N_HEADS:]], axis=1)
    fcols = w_in_l[:, a:a + N_HEADS]
    pad = jnp.zeros((d, N_IN_PAD - C_FXF - N_HEADS), w_in_l.dtype)
    return jnp.concatenate([main, fcols, pad], axis=1).astype(F32)


def _seg_matrix():
    r = lax.broadcasted_iota(jnp.int32, (ATT_W, ATT_W), 0) // HEAD_DIM
    c = lax.broadcasted_iota(jnp.int32, (ATT_W, ATT_W), 1) // HEAD_DIM
    return jnp.where(r == c, 1.0, 0.0).astype(BF16)


def _row(v, width=None):
    v = v.reshape(1, -1).astype(F32)
    if width is not None and v.shape[1] < width:
        v = jnp.pad(v, ((0, 0), (0, width - v.shape[1])))
    return v


def kernel(x_prompt, x_sample, cache_sb_k, cache_sb_v, cache_fox_k, cache_fox_v, cache_fox_logf, state_hgrn,
           page_table, w_in, w_out, norm_mix_g, norm_ffn_g, fox_q_norm_g, fox_k_norm_g, fox_f_bias, sb_out_g,
           fox_out_g, hgrn_out_g, hgrn_lb_logits, ffn_w_gate, ffn_w_up, ffn_w_down, moe_router_w,
           moe_router_b, moe_w_gate, moe_w_up, moe_w_down):
    batch, seq, d_model = x_prompt.shape
    dec_b = x_sample.shape[0]
    depth = w_in.shape[0]
    n_pool, page = cache_sb_k.shape[1], cache_sb_k.shape[2]
    n_experts = moe_router_w.shape[-1]

    lb_all = lower_bounds(hgrn_lb_logits.astype(F32))
    seg = _seg_matrix()
    sbk_c = cache_sb_k.reshape(depth * n_pool, page, ATT_W)
    sbv_c = cache_sb_v.reshape(depth * n_pool, page, ATT_W)
    fxk_c = cache_fox_k.reshape(depth * n_pool, page, ATT_W)
    fxv_c = cache_fox_v.reshape(depth * n_pool, page, ATT_W)
    fxf_c = jnp.swapaxes(cache_fox_logf, 2, 3).reshape(depth * n_pool, N_HEADS, page)

    xp = x_prompt.reshape(batch * seq, d_model)
    xs = x_sample.reshape(dec_b, d_model)
    p_out = [[] for _ in range(6)]
    s_out = [[] for _ in range(6)]

    for l in range(depth):
        w_in_f = _arrange_w_in(w_in[l])
        w_in_l = w_in_f.astype(BF16)
        w_out_f = w_out[l].astype(F32)
        w_out_l = w_out_f.astype(BF16)
        g_mix, g_ffn = _row(norm_mix_g[l]), _row(norm_ffn_g[l])
        gq = _row(jnp.tile(fox_q_norm_g[l], N_HEADS))
        gk = _row(jnp.tile(fox_k_norm_g[l], N_HEADS))
        fb = _row(fox_f_bias[l], LANES)
        g_sb, g_fx, g_hg = _row(sb_out_g[l]), _row(fox_out_g[l]), _row(hgrn_out_g[l])
        lb_row = lb_all[l:l + 1]

        p = in_proj(xp, g_mix, w_in_l, tm=512, tn=1280)
        fxk_n, lf, qa, ka = fox_prep(p, gq, gk, fb, seg, batch=batch, seq=seq, tb=256)
        sb_o = sb_attention(p, g_sb, batch=batch, seq=seq, tq=256)
        fx_o = fox_attention(qa, ka, p, g_fx, batch=batch, seq=seq, tq=256)
        hg_o, s_fin = hgrn_prompt(p, lb_row, g_hg, batch=batch, seq=seq, chunk=128, sub=16)
        hp = out_proj(xp, sb_o, fx_o, hg_o, w_out_l, tm=512)
        p_out[0].append(p[:, C_SBK:C_SBK + ATT_W])
        p_out[1].append(p[:, C_SBV:C_SBV + ATT_W])
        p_out[2].append(fxk_n)
        p_out[3].append(p[:, C_FXV:C_FXV + ATT_W])
        p_out[4].append(lf)
        p_out[5].append(s_fin)

        ps = in_proj(xs, g_mix, w_in_f, tm=dec_b, tn=1280)
        tok = lambda c0, w: ps[:, c0:c0 + w].reshape(dec_b, 1, w)
        off = l * n_pool
        sb_os = sb_decode(tok(C_SBQ, ATT_W), g_sb, sbk_c, sbv_c, page_table, page_off=off, pps=16)
        fx_os, fxk_s, lf_s = fox_decode(tok(C_FXQ, ATT_W), tok(C_FXK, ATT_W), tok(C_FXV, ATT_W),
                                        tok(C_FXF, LANES), gq, gk, fb, g_fx, fxk_c, fxv_c, fxf_c,
                                        page_table, page_off=off, pps=16)
        hg_os, s_new = hgrn_decode(tok(C_HGQ, HG_W), tok(C_HGF, HG_W), tok(C_HGI, HG_W), tok(C_HGG, HG_W),
                                   lb_row, g_hg, state_hgrn[l].astype(F32))
        hs = out_proj(xs, sb_os.reshape(dec_b, ATT_W), fx_os.reshape(dec_b, ATT_W),
                      hg_os.reshape(dec_b, HG_W), w_out_f, tm=dec_b)
        s_out[0].append(ps[:, C_SBK:C_SBK + ATT_W])
        s_out[1].append(ps[:, C_SBV:C_SBV + ATT_W])
        s_out[2].append(fxk_s.reshape(dec_b, ATT_W))
        s_out[3].append(ps[:, C_FXV:C_FXV + ATT_W])
        s_out[4].append(lf_s.reshape(dec_b, LANES)[:, :N_HEADS])
        s_out[5].append(s_new)

        if l % 2 == 0:
            wf = [w[l // 2].astype(F32) for w in (ffn_w_gate, ffn_w_up, ffn_w_down)]
            wg, wu, wd = (w.astype(BF16) for w in wf)
            xp = ffn_dense(hp, g_ffn, wg, wu, wd, tm=1024, tf=256)
            xs = ffn_dense(hs, g_ffn, *wf, tm=dec_b, tf=256)
        else:
            wf = [w[l // 2].astype(F32) for w in (moe_w_gate, moe_w_up, moe_w_down)]
            wg, wu, wd = (w.astype(BF16) for w in wf)
            rw = jnp.pad(moe_router_w[l // 2].astype(F32), ((0, 0), (0, LANES - n_experts)))
            rb = _row(moe_router_b[l // 2], LANES)
            gates_p = moe_router(hp, g_ffn, rw, rb, tm=512, n_experts=n_experts)
            xp = moe_ffn(hp, g_ffn, gates_p, wg, wu, wd, tm=512)
            gates_s = moe_router(hs, g_ffn, rw, rb, tm=dec_b, n_experts=n_experts)
            xs = moe_ffn(hs, g_ffn, gates_s, *wf, tm=dec_b)

    def stack(parts, shape):
        return jnp.stack(parts).reshape((depth,) + shape)

    hd = (N_HEADS, HEAD_DIM)
    y_prompt = xp.reshape(batch, seq, d_model)
    y_sample = xs.reshape(dec_b, 1, d_model)
    outs_p = [stack(p_out[i], (batch, seq) + hd) for i in range(4)]
    outs_p.append(stack(p_out[4], (batch, seq, N_HEADS)))
    outs_p.append(stack(p_out[5], (batch, N_HEADS, HG_D, HG_D)).astype(state_hgrn.dtype))
    outs_s = [stack(s_out[i], (dec_b, 1) + hd) for i in range(4)]
    outs_s.append(stack(s_out[4], (dec_b, 1, N_HEADS)))
    outs_s.append(stack(s_out[5], (dec_b, N_HEADS, HG_D, HG_D)).astype(state_hgrn.dtype))
    return (y_prompt, y_sample, *outs_p, *outs_s)
```

```python
import functools

import jax
import jax.numpy as jnp
from jax import lax
from jax.experimental import pallas as pl
from jax.experimental.pallas import tpu as pltpu

F32 = jnp.float32
BF16 = jnp.bfloat16
EPS = 1e-6
NEG_BIG = -1e30

HEAD_DIM = 64
N_HEADS = 4
ATT_W = N_HEADS * HEAD_DIM
HG_D = 128
HG_W = N_HEADS * HG_D
LANES = 128
VMEM_LIMIT = 56 * 1024 * 1024

C_SBQ, C_SBK, C_SBV = 0, 256, 512
C_FXQ, C_FXK, C_FXV = 768, 1024, 1280
C_HGQ, C_HGF, C_HGI, C_HGG = 1536, 2048, 2560, 3072
C_FXF = 3584
N_IN_PAD = 3840


def _cparams(*sem):
    return pltpu.CompilerParams(dimension_semantics=sem, vmem_limit_bytes=VMEM_LIMIT)


def _rms(xf, g_row):
    ms = jnp.mean(xf * xf, axis=-1, keepdims=True)
    return xf * lax.rsqrt(ms + EPS) * g_row


def _softplus(z):
    return jnp.maximum(z, 0.0) + jnp.log(1.0 + jnp.exp(-jnp.abs(z)))


def _sigmoid(x):
    return 1.0 / (1.0 + jnp.exp(-x))


def _split2(x):
    hi = x.astype(BF16)
    lo = (x - hi.astype(F32)).astype(BF16)
    return hi, lo


def _split3(x):
    hi = x.astype(BF16)
    r = x - hi.astype(F32)
    mid = r.astype(BF16)
    lo = (r - mid.astype(F32)).astype(BF16)
    return hi, mid, lo


def _dot(a, b):
    return jnp.dot(a, b, preferred_element_type=F32)


def _dot_nt(a, b):
    return lax.dot_general(a, b, (((1,), (1,)), ((), ())), preferred_element_type=F32)


def _stack_split2(x):
    hi, lo = _split2(x)
    return hi, jnp.concatenate([hi, lo], axis=0)


def _dot3(x, w):
    m = x.shape[0]
    xh, x2 = _stack_split2(x)
    wh, wl = _split2(w)
    r = _dot(x2, wh)
    return r[:m] + r[m:] + _dot(xh, wl)


def _dot3_nt(x, w):
    m = x.shape[0]
    xh, x2 = _stack_split2(x)
    wh, wl = _split2(w)
    r = _dot_nt(x2, wh)
    return r[:m] + r[m:] + _dot_nt(xh, wl)


def _dot_ones(x, ones_bf16):
    m = x.shape[0]
    r = _dot(jnp.concatenate(_split3(x), axis=0), ones_bf16)
    return r[:m] + r[m:2 * m] + r[2 * m:]


def _mm(x, w, precise):
    return _dot3(x, w) if precise else _dot(x.astype(BF16), w)


def _tri(n, rel):
    r = lax.broadcasted_iota(jnp.int32, (n, n), 0)
    c = lax.broadcasted_iota(jnp.int32, (n, n), 1)
    return jnp.where(rel(r, c), 1.0, 0.0).astype(BF16)


def _lower_bounds_kernel(x_ref, o_ref):
    x = x_ref[...]
    depth = x.shape[0]
    m = jnp.max(x, axis=0, keepdims=True)
    e = jnp.exp(x - m)
    sm = e / jnp.sum(e, axis=0, keepdims=True)
    run = jnp.zeros_like(sm[0:1])
    rows = []
    for l in range(depth):
        run = run + sm[l:l + 1]
        rows.append(run - sm[0:1])
    o_ref[...] = jnp.concatenate(rows, axis=0)


def lower_bounds(logits):
    return pl.pallas_call(
        _lower_bounds_kernel,
        out_shape=jax.ShapeDtypeStruct(logits.shape, F32),
        name="hgrn_lower_bounds",
    )(logits)


def _in_proj_kernel(x_ref, g_ref, w_ref, o_ref, xn_ref, *, precise):
    @pl.when(pl.program_id(1) == 0)
    def _():
        xn_ref[...] = _rms(x_ref[...], g_ref[...]).astype(xn_ref.dtype)

    o_ref[...] = _mm(xn_ref[...], w_ref[...], precise)


def in_proj(x, g_row, w, *, tm, tn):
    n, d = x.shape
    n_out = w.shape[1]
    precise = w.dtype == F32
    return pl.pallas_call(
        functools.partial(_in_proj_kernel, precise=precise),
        grid=(n // tm, n_out // tn),
        in_specs=[pl.BlockSpec((tm, d), lambda i, j: (i, 0)),
                  pl.BlockSpec((1, d), lambda i, j: (0, 0)),
                  pl.BlockSpec((d, tn), lambda i, j: (0, j))],
        out_specs=pl.BlockSpec((tm, tn), lambda i, j: (i, j)),
        out_shape=jax.ShapeDtypeStruct((n, n_out), F32),
        scratch_shapes=[pltpu.VMEM((tm, d), F32 if precise else BF16)],
        compiler_params=_cparams("parallel", "arbitrary"),
        name="norm_in_proj",
    )(x, g_row, w)


def _seg_mean_sq(x, seg_ref):
    hi, lo = _split2(x * x)
    return (_dot(hi, seg_ref[...]) + _dot(lo, seg_ref[...])) * (1.0 / HEAD_DIM)


def _fox_prep_kernel(q_ref, k_ref, f_ref, gq_ref, gk_ref, fb_ref, seg_ref,
                     kn_ref, lf_ref, qa_ref, ka_ref, carry_ref):
    tb = q_ref.shape[0]

    @pl.when(pl.program_id(1) == 0)
    def _():
        carry_ref[...] = jnp.zeros_like(carry_ref)

    xq = q_ref[...]
    xk = k_ref[...]
    qn = xq * lax.rsqrt(_seg_mean_sq(xq, seg_ref) + EPS) * gq_ref[...]
    kn = xk * lax.rsqrt(_seg_mean_sq(xk, seg_ref) + EPS) * gk_ref[...]
    kn_ref[...] = kn

    fx = f_ref[...] + fb_ref[...]
    lf = jnp.minimum(fx, 0.0) - jnp.log(1.0 + jnp.exp(-jnp.abs(fx)))
    lf_ref[...] = lf[:, :N_HEADS]

    ltri = _tri(tb, lambda r, c: c <= r)
    p0, p1, p2 = _split3(lf)
    cum = _dot(ltri, p0) + _dot(ltri, p1) + _dot(ltri, p2) + carry_ref[...]
    carry_ref[...] = cum[tb - 1:tb, :]

    c0, c1, c2 = (c.astype(F32) for c in _split3(cum))
    lane = lax.broadcasted_iota(jnp.int32, (tb, HEAD_DIM), 1)
    one = jnp.where(lane < 6, 1.0, 0.0)
    qs = qn * (HEAD_DIM ** -0.5)
    q_parts, k_parts = [], []
    for h in range(N_HEADS):
        a0, a1, a2 = c0[:, h:h + 1], c1[:, h:h + 1], c2[:, h:h + 1]
        ext_q = jnp.where(lane == 0, a0, jnp.where(lane == 1, a1, jnp.where(lane == 2, a2, one)))
        ext_k = jnp.where(lane == 3, -a0, jnp.where(lane == 4, -a1, jnp.where(lane == 5, -a2, one)))
        q_parts += [qs[:, h * HEAD_DIM:(h + 1) * HEAD_DIM], ext_q]
        k_parts += [kn[:, h * HEAD_DIM:(h + 1) * HEAD_DIM], ext_k]
    qa_ref[...] = jnp.concatenate(q_parts, axis=1).astype(BF16)
    ka_ref[...] = jnp.concatenate(k_parts, axis=1).astype(BF16)


def fox_prep(p, gq_row, gk_row, fb_row, seg, *, batch, seq, tb):
    n = p.shape[0]
    nb = seq // tb
    row = lambda b, t: b * nb + t
    aw = N_HEADS * LANES
    return pl.pallas_call(
        _fox_prep_kernel,
        grid=(batch, nb),
        in_specs=[pl.BlockSpec((tb, ATT_W), lambda b, t: (row(b, t), C_FXQ // ATT_W)),
                  pl.BlockSpec((tb, ATT_W), lambda b, t: (row(b, t), C_FXK // ATT_W)),
                  pl.BlockSpec((tb, LANES), lambda b, t: (row(b, t), C_FXF // LANES)),
                  pl.BlockSpec((1, ATT_W), lambda b, t: (0, 0)),
                  pl.BlockSpec((1, ATT_W), lambda b, t: (0, 0)),
                  pl.BlockSpec((1, LANES), lambda b, t: (0, 0)),
                  pl.BlockSpec((ATT_W, ATT_W), lambda b, t: (0, 0))],
        out_specs=[pl.BlockSpec((tb, ATT_W), lambda b, t: (row(b, t), 0)),
                   pl.BlockSpec((tb, N_HEADS), lambda b, t: (row(b, t), 0)),
                   pl.BlockSpec((tb, aw), lambda b, t: (row(b, t), 0)),
                   pl.BlockSpec((tb, aw), lambda b, t: (row(b, t), 0))],
        out_shape=[jax.ShapeDtypeStruct((n, ATT_W), F32),
                   jax.ShapeDtypeStruct((n, N_HEADS), F32),
                   jax.ShapeDtypeStruct((n, aw), BF16),
                   jax.ShapeDtypeStruct((n, aw), BF16)],
        scratch_shapes=[pltpu.VMEM((1, LANES), F32)],
        compiler_params=_cparams("parallel", "arbitrary"),
        name="fox_prep",
    )(p, p, p, gq_row, gk_row, fb_row, seg)


def _store_v_transposed(v_ref, vt_ref, tk):
    for j in range(vt_ref.shape[0]):
        vt_ref[j] = v_ref[j * tk:(j + 1) * tk, :].T.astype(BF16)


def _finish_attention(ot, g_ref, o_ref):
    ms = jnp.mean(ot * ot, axis=0, keepdims=True)
    o_ref[...] = (ot * lax.rsqrt(ms + EPS)).T * g_ref[...]


def _sb_attn_kernel(q_ref, k_ref, v_ref, g_ref, o_ref, acc_ref, r_ref, *, tk):
    i = pl.program_id(1)
    tq = q_ref.shape[0]
    acc_ref[...] = jnp.zeros_like(acc_ref)
    r_ref[...] = jnp.zeros_like(r_ref)
    u = _tri(tk, lambda r, c: r >= c)
    n_diag = tq // tk
    qry_i = lax.broadcasted_iota(jnp.int32, (tq, tk), 0)
    key_i = lax.broadcasted_iota(jnp.int32, (tq, tk), 1)
    low_half = lax.broadcasted_iota(jnp.int32, (tq, LANES), 1) < HEAD_DIM
    qm = []
    for pair in range(N_HEADS // 2):
        qp = q_ref[:, pair * LANES:(pair + 1) * LANES] * (HEAD_DIM ** -0.5)
        qm.append(jnp.where(low_half, qp, 0.0).astype(BF16))
        qm.append(jnp.where(low_half, 0.0, qp).astype(BF16))

    def block(j, diag_off):
        row0 = pl.multiple_of(j * tk, tk)
        before = None if diag_off is None else (key_i + diag_off < qry_i)
        for pair in range(N_HEADS // 2):
            ls = slice(pair * LANES, (pair + 1) * LANES)
            kp = k_ref[pl.ds(row0, tk), ls].astype(BF16)
            vp = v_ref[pl.ds(row0, tk), ls].astype(BF16)
            res = []
            for h in (2 * pair, 2 * pair + 1):
                z = _dot_nt(qm[h], kp)
                sp = _softplus(z)
                if before is not None:
                    sp = jnp.where(before, sp, 0.0)
                cs = _dot(sp.astype(BF16), u)
                r_old = r_ref[h]
                a = jnp.exp(z - cs - r_old)
                if before is not None:
                    a = jnp.where(before, a, 0.0)
                res.append(_dot(a.astype(BF16), vp))
                r_ref[h] = r_old + cs[:, 0:1]
            acc_ref[:, ls] += jnp.where(low_half, res[0], res[1])

    for d in reversed(range(n_diag)):
        block(i * n_diag + d, d * tk)

    def body(jj, c):
        block(i * n_diag - 1 - jj, None)
        return c

    lax.fori_loop(0, i * n_diag, body, 0)
    o_ref[...] = _rms(acc_ref[...], g_ref[...])


def sb_attention(p, g_row, *, batch, seq, tq, tk):
    n = p.shape[0]
    nq = seq // tq
    assert tq % tk == 0
    kern = functools.partial(_sb_attn_kernel, tk=tk)
    return pl.pallas_call(
        kern,
        grid=(batch, nq),
        in_specs=[pl.BlockSpec((tq, ATT_W), lambda b, i: (b * nq + i, C_SBQ // ATT_W)),
                  pl.BlockSpec((seq, ATT_W), lambda b, i: (b, C_SBK // ATT_W)),
                  pl.BlockSpec((seq, ATT_W), lambda b, i: (b, C_SBV // ATT_W)),
                  pl.BlockSpec((1, ATT_W), lambda b, i: (0, 0))],
        out_specs=pl.BlockSpec((tq, ATT_W), lambda b, i: (b * nq + i, 0)),
        out_shape=jax.ShapeDtypeStruct((n, ATT_W), F32),
        scratch_shapes=[pltpu.VMEM((tq, ATT_W), F32),
                        pltpu.VMEM((N_HEADS, tq, 1), F32)],
        compiler_params=_cparams("parallel", "arbitrary"),
        name="sb_attention",
    )(p, p, p, g_row)


def _fox_attn_kernel(q_ref, k_ref, v_ref, g_ref, o_ref, vt_ref, acc_ref, m_ref, l_ref, *, tk):
    i = pl.program_id(1)
    tq = q_ref.shape[0]

    @pl.when(i == 0)
    def _():
        _store_v_transposed(v_ref, vt_ref, tk)

    acc_ref[...] = jnp.zeros_like(acc_ref)
    m_ref[...] = jnp.full_like(m_ref, NEG_BIG)
    l_ref[...] = jnp.zeros_like(l_ref)
    key_i = lax.broadcasted_iota(jnp.int32, (tk, tq), 0)
    qry_i = lax.broadcasted_iota(jnp.int32, (tk, tq), 1)
    causal = key_i <= qry_i

    def block(j, diagonal):
        row0 = pl.multiple_of(j * tk, tk)
        sts = [_dot_nt(k_ref[pl.ds(row0, tk), h * LANES:(h + 1) * LANES],
                       q_ref[:, h * LANES:(h + 1) * LANES]) for h in range(N_HEADS)]
        for h in range(N_HEADS):
            hs = slice(h * HEAD_DIM, (h + 1) * HEAD_DIM)
            st = sts[h]
            if diagonal:
                st = jnp.where(causal, st, NEG_BIG)
            m_old = m_ref[h:h + 1, :]
            m_new = jnp.maximum(m_old, jnp.max(st, axis=0, keepdims=True))
            alpha = jnp.exp(m_old - m_new)
            pt = jnp.exp(st - m_new)
            l_ref[h:h + 1, :] = alpha * l_ref[h:h + 1, :] + jnp.sum(pt, axis=0, keepdims=True)
            acc_ref[hs, :] = alpha * acc_ref[hs, :] + _dot(vt_ref[j, hs, :], pt.astype(BF16))
            m_ref[h:h + 1, :] = m_new

    block(i, True)

    def body(jj, c):
        block(i - 1 - jj, False)
        return c

    lax.fori_loop(0, i, body, 0)
    parts = [acc_ref[h * HEAD_DIM:(h + 1) * HEAD_DIM, :] / l_ref[h:h + 1, :] for h in range(N_HEADS)]
    _finish_attention(jnp.concatenate(parts, axis=0), g_ref, o_ref)


def fox_attention(qa, ka, p, g_row, *, batch, seq, tq):
    n = p.shape[0]
    nq = seq // tq
    aw = N_HEADS * LANES
    kern = functools.partial(_fox_attn_kernel, tk=tq)
    return pl.pallas_call(
        kern,
        grid=(batch, nq),
        in_specs=[pl.BlockSpec((tq, aw), lambda b, i: (b * nq + i, 0)),
                  pl.BlockSpec((seq, aw), lambda b, i: (b, 0)),
                  pl.BlockSpec((seq, ATT_W), lambda b, i: (b, C_FXV // ATT_W)),
                  pl.BlockSpec((1, ATT_W), lambda b, i: (0, 0))],
        out_specs=pl.BlockSpec((tq, ATT_W), lambda b, i: (b * nq + i, 0)),
        out_shape=jax.ShapeDtypeStruct((n, ATT_W), F32),
        scratch_shapes=[pltpu.VMEM((nq, ATT_W, tq), BF16),
                        pltpu.VMEM((ATT_W, tq), F32),
                        pltpu.VMEM((8, tq), F32),
                        pltpu.VMEM((8, tq), F32)],
        compiler_params=_cparams("parallel", "arbitrary"),
        name="fox_attention",
    )(qa, ka, p, g_row)


def _hgrn_gates(hq, hf, lb):
    q = hq * _sigmoid(hq)
    f = lb + (1.0 - lb) * _sigmoid(hf)
    return q, f, 1.0 - f


def _hgrn_kernel(hq_ref, hf_ref, hi_ref, hg_ref, lb_ref, g_ref, o_ref, s_ref, st_ref, *, sub):
    c = hq_ref.shape[0]
    n_sub = c // sub

    @pl.when(pl.program_id(1) == 0)
    def _():
        st_ref[...] = jnp.zeros_like(st_ref)

    ltri = _tri(c, lambda r, cc: cc <= r)
    t_i = lax.broadcasted_iota(jnp.int32, (sub, sub), 0)
    s_i = lax.broadcasted_iota(jnp.int32, (sub, sub), 1)
    outs = []
    for h in range(N_HEADS):
        ws = slice(h * HG_D, (h + 1) * HG_D)
        q, f, k = _hgrn_gates(hq_ref[:, ws], hf_ref[:, ws], lb_ref[:, ws])
        v = hi_ref[:, ws]
        vb = v.astype(BF16)
        p0, p1, p2 = _split3(jnp.log(f))
        g = _dot(ltri, p0) + _dot(ltri, p1) + _dot(ltri, p2)
        st = st_ref[h]
        o_inter = _dot_nt((q * jnp.exp(g)).astype(BF16), st.astype(BF16))
        rows = []
        for blk in range(n_sub):
            rs = slice(blk * sub, (blk + 1) * sub)
            qb, gb, kb = q[rs], g[rs], k[rs]
            d = jnp.zeros((sub, sub), F32)
            for s in range(sub):
                dec = jnp.exp(jnp.minimum(gb - gb[s:s + 1, :], 0.0))
                col = jnp.sum(qb * kb[s:s + 1, :] * dec, axis=-1, keepdims=True)
                d = jnp.where(s_i == s, col, d)
            d = jnp.where(s_i <= t_i, d, 0.0)
            o_b = o_inter[rs] + _dot(d.astype(BF16), vb[rs])
            if blk > 0:
                lo = blk * sub
                g_ref_row = g[lo - 1:lo, :]
                qs = (qb * jnp.exp(gb - g_ref_row)).astype(BF16)
                ks = (k[:lo] * jnp.exp(g_ref_row - g[:lo])).astype(BF16)
                o_b = o_b + _dot(_dot_nt(qs, ks).astype(BF16), vb[:lo])
            rows.append(o_b)
        o = jnp.concatenate(rows, axis=0)
        g_end = g[c - 1:c, :]
        k_end = (k * jnp.exp(g_end - g)).astype(BF16)
        st_ref[h] = st * jnp.exp(g_end) + _dot(v.T.astype(BF16), k_end)
        outs.append(_rms(o, g_ref[...]) * _sigmoid(hg_ref[:, ws]))
    o_ref[...] = jnp.concatenate(outs, axis=1)

    @pl.when(pl.program_id(1) == pl.num_programs(1) - 1)
    def _():
        for h in range(N_HEADS):
            s_ref[0, h] = st_ref[h].T


def hgrn_prompt(p, lb_row, g_row, *, batch, seq, chunk, sub):
    n = p.shape[0]
    nc = seq // chunk
    col = lambda c0: (lambda b, t: (b * nc + t, c0 // HG_W))
    kern = functools.partial(_hgrn_kernel, sub=sub)
    return pl.pallas_call(
        kern,
        grid=(batch, nc),
        in_specs=[pl.BlockSpec((chunk, HG_W), col(C_HGQ)),
                  pl.BlockSpec((chunk, HG_W), col(C_HGF)),
                  pl.BlockSpec((chunk, HG_W), col(C_HGI)),
                  pl.BlockSpec((chunk, HG_W), col(C_HGG)),
                  pl.BlockSpec((1, HG_W), lambda b, t: (0, 0)),
                  pl.BlockSpec((1, HG_D), lambda b, t: (0, 0))],
        out_specs=[pl.BlockSpec((chunk, HG_W), lambda b, t: (b * nc + t, 0)),
                   pl.BlockSpec((1, N_HEADS, HG_D, HG_D), lambda b, t: (b, 0, 0, 0))],
        out_shape=[jax.ShapeDtypeStruct((n, HG_W), F32),
                   jax.ShapeDtypeStruct((batch, N_HEADS, HG_D, HG_D), F32)],
        scratch_shapes=[pltpu.VMEM((N_HEADS, HG_D, HG_D), F32)],
        compiler_params=_cparams("parallel", "arbitrary"),
        name="hgrn_prompt",
    )(p, p, p, p, lb_row, g_row)


def _out_proj_kernel(x_ref, a_ref, b_ref, c_ref, w_ref, o_ref, *, precise):
    mix = jnp.concatenate([a_ref[...], b_ref[...], c_ref[...]], axis=1)
    o_ref[...] = x_ref[...] + _mm(mix, w_ref[...], precise)


def out_proj(x, sb_o, fx_o, hg_o, w, *, tm):
    n, d = x.shape
    row = lambda i: (i, 0)
    return pl.pallas_call(
        functools.partial(_out_proj_kernel, precise=w.dtype == F32),
        grid=(n // tm,),
        in_specs=[pl.BlockSpec((tm, d), row),
                  pl.BlockSpec((tm, ATT_W), row),
                  pl.BlockSpec((tm, ATT_W), row),
                  pl.BlockSpec((tm, HG_W), row),
                  pl.BlockSpec(w.shape, lambda i: (0, 0))],
        out_specs=pl.BlockSpec((tm, d), row),
        out_shape=jax.ShapeDtypeStruct((n, d), F32),
        compiler_params=_cparams("parallel"),
        name="out_proj",
    )(x, sb_o, fx_o, hg_o, w)


def _ffn_kernel(h_ref, g_ref, wg_ref, wu_ref, wd_ref, o_ref, hn_ref, acc_ref, *, precise):
    j = pl.program_id(1)

    @pl.when(j == 0)
    def _():
        hn_ref[...] = _rms(h_ref[...], g_ref[...]).astype(hn_ref.dtype)
        acc_ref[...] = jnp.zeros_like(acc_ref)

    hn = hn_ref[...]
    gate = _mm(hn, wg_ref[...], precise)
    up = _mm(hn, wu_ref[...], precise)
    acc_ref[...] += _mm(gate * _sigmoid(gate) * up, wd_ref[...], precise)

    @pl.when(j == pl.num_programs(1) - 1)
    def _():
        o_ref[...] = h_ref[...] + acc_ref[...]


def ffn_dense(h, g_row, wg, wu, wd, *, tm, tf):
    n, d = h.shape
    ff = wg.shape[1]
    precise = wg.dtype == F32
    return pl.pallas_call(
        functools.partial(_ffn_kernel, precise=precise),
        grid=(n // tm, ff // tf),
        in_specs=[pl.BlockSpec((tm, d), lambda i, j: (i, 0)),
                  pl.BlockSpec((1, d), lambda i, j: (0, 0)),
                  pl.BlockSpec((d, tf), lambda i, j: (0, j)),
                  pl.BlockSpec((d, tf), lambda i, j: (0, j)),
                  pl.BlockSpec((tf, d), lambda i, j: (j, 0))],
        out_specs=pl.BlockSpec((tm, d), lambda i, j: (i, 0)),
        out_shape=jax.ShapeDtypeStruct((n, d), F32),
        scratch_shapes=[pltpu.VMEM((tm, d), F32 if precise else BF16), pltpu.VMEM((tm, d), F32)],
        compiler_params=_cparams("parallel", "arbitrary"),
        name="ffn_dense",
    )(h, g_row, wg, wu, wd)


def _router_kernel(h_ref, g_ref, w_ref, b_ref, o_ref, *, n_experts):
    hn = _rms(h_ref[...], g_ref[...])
    xh, xl = _split2(hn)
    wh, wl = _split2(w_ref[...])
    logits = _dot(xh, wh) + _dot(xl, wh) + _dot(xh, wl) + b_ref[...]
    lane = lax.broadcasted_iota(jnp.int32, logits.shape, 1).astype(F32)
    logits = jnp.where(lane < n_experts, logits, NEG_BIG)
    m1 = jnp.max(logits, axis=1, keepdims=True)
    i1 = jnp.min(jnp.where(logits == m1, lane, float(LANES)), axis=1, keepdims=True)
    first = lane == i1
    rest = jnp.where(first, NEG_BIG, logits)
    m2 = jnp.max(rest, axis=1, keepdims=True)
    i2 = jnp.min(jnp.where(rest == m2, lane, float(LANES)), axis=1, keepdims=True)
    second = lane == i2
    e = jnp.exp(m2 - m1)
    w1 = 1.0 / (1.0 + e)
    o_ref[...] = jnp.where(first, w1, 0.0) + jnp.where(second, e * w1, 0.0)


def moe_router(h, g_row, w_pad, b_pad, *, tm, n_experts):
    n, d = h.shape
    kern = functools.partial(_router_kernel, n_experts=n_experts)
    return pl.pallas_call(
        kern,
        grid=(n // tm,),
        in_specs=[pl.BlockSpec((tm, d), lambda i: (i, 0)),
                  pl.BlockSpec((1, d), lambda i: (0, 0)),
                  pl.BlockSpec((d, LANES), lambda i: (0, 0)),
                  pl.BlockSpec((1, LANES), lambda i: (0, 0))],
        out_specs=pl.BlockSpec((tm, LANES), lambda i: (i, 0)),
        out_shape=jax.ShapeDtypeStruct((n, LANES), F32),
        compiler_params=_cparams("parallel"),
        name="moe_router",
    )(h, g_row, w_pad, b_pad)


def _moe_kernel(h_ref, g_ref, gates_ref, wg_ref, wu_ref, wd_ref, o_ref, hn_ref, acc_ref, *, precise):
    e = pl.program_id(1)

    @pl.when(e == 0)
    def _():
        hn_ref[...] = _rms(h_ref[...], g_ref[...]).astype(hn_ref.dtype)
        acc_ref[...] = jnp.zeros_like(acc_ref)

    gates = gates_ref[...]
    lane = lax.broadcasted_iota(jnp.int32, gates.shape, 1)
    ge = jnp.sum(jnp.where(lane == e, gates, 0.0), axis=1, keepdims=True)
    hn = hn_ref[...]
    gate = _mm(hn, wg_ref[0], precise)
    up = _mm(hn, wu_ref[0], precise)
    acc_ref[...] += _mm(gate * _sigmoid(gate) * up * ge, wd_ref[0], precise)

    @pl.when(e == pl.num_programs(1) - 1)
    def _():
        o_ref[...] = h_ref[...] + acc_ref[...]


def moe_ffn(h, g_row, gates, wg, wu, wd, *, tm):
    n, d = h.shape
    n_e, _, ffe = wg.shape
    precise = wg.dtype == F32
    return pl.pallas_call(
        functools.partial(_moe_kernel, precise=precise),
        grid=(n // tm, n_e),
        in_specs=[pl.BlockSpec((tm, d), lambda i, e: (i, 0)),
                  pl.BlockSpec((1, d), lambda i, e: (0, 0)),
                  pl.BlockSpec((tm, LANES), lambda i, e: (i, 0)),
                  pl.BlockSpec((1, d, ffe), lambda i, e: (e, 0, 0)),
                  pl.BlockSpec((1, d, ffe), lambda i, e: (e, 0, 0)),
                  pl.BlockSpec((1, ffe, d), lambda i, e: (e, 0, 0))],
        out_specs=pl.BlockSpec((tm, d), lambda i, e: (i, 0)),
        out_shape=jax.ShapeDtypeStruct((n, d), F32),
        scratch_shapes=[pltpu.VMEM((tm, d), F32 if precise else BF16), pltpu.VMEM((tm, d), F32)],
        compiler_params=_cparams("parallel", "arbitrary"),
        name="moe_ffn",
    )(h, g_row, gates, wg, wu, wd)


def _head_rows(width):
    r = lax.broadcasted_iota(jnp.int32, (8, width), 0)
    c = lax.broadcasted_iota(jnp.int32, (8, width), 1)
    return (c // HEAD_DIM) == r


def _lane_to_rows(row_vec):
    r = lax.broadcasted_iota(jnp.int32, (8, LANES), 0)
    c = lax.broadcasted_iota(jnp.int32, (8, LANES), 1)
    return jnp.sum(jnp.where(r == c, row_vec, 0.0), axis=1, keepdims=True)


def _finish_decode(acc, own, g_ref, o_ref):
    o = jnp.sum(jnp.where(own, acc, 0.0), axis=0, keepdims=True)
    o_ref[0] = _rms(o, g_ref[...])


def _sb_decode_kernel(pt_ref, q_ref, g_ref, *rest, pps):
    k_refs, v_refs = rest[:pps], rest[pps:2 * pps]
    o_ref, acc_ref, r_ref = rest[2 * pps:]
    c = pl.program_id(1)

    @pl.when(c == 0)
    def _():
        acc_ref[...] = jnp.zeros_like(acc_ref)
        r_ref[...] = jnp.zeros_like(r_ref)

    own = _head_rows(ATT_W)
    qb = jnp.where(own, q_ref[0] * (HEAD_DIM ** -0.5), 0.0)
    u = _tri(LANES, lambda r, cc: r >= cc)
    order = list(reversed(range(pps)))
    zs = [_dot3(qb, k_refs[i][0]) for i in order]
    css = [_dot_ones(_softplus(z), u) for z in zs]
    acc = acc_ref[...]
    run = r_ref[...]
    for i, z, cs in zip(order, zs, css):
        acc = acc + _dot3_nt(jnp.exp(z - cs - run), v_refs[i][0])
        run = run + cs[:, 0:1]
    acc_ref[...] = acc
    r_ref[...] = run

    @pl.when(c == pl.num_programs(1) - 1)
    def _():
        _finish_decode(acc, own, g_ref, o_ref)


def _page_specs(n_chunks, pps, page_off, block):
    def spec(i):
        def index(b, c, pt):
            return (page_off + pt[b, (n_chunks - 1 - c) * pps + i],) + (0,) * (len(block) - 1)
        return pl.BlockSpec(block, index)
    return [spec(i) for i in range(pps)]


def sb_decode(q3, g_row, cache_k, cache_v, page_table, *, page_off, pps):
    nb = q3.shape[0]
    n_pages = page_table.shape[1]
    n_chunks = n_pages // pps
    page = cache_k.shape[2]
    kern = functools.partial(_sb_decode_kernel, pps=pps)
    pages = _page_specs(n_chunks, pps, page_off, (1, ATT_W, page))
    grid_spec = pltpu.PrefetchScalarGridSpec(
        num_scalar_prefetch=1,
        grid=(nb, n_chunks),
        in_specs=[pl.BlockSpec((1, 1, ATT_W), lambda b, c, pt: (b, 0, 0)),
                  pl.BlockSpec((1, ATT_W), lambda b, c, pt: (0, 0))] + pages + pages,
        out_specs=pl.BlockSpec((1, 1, ATT_W), lambda b, c, pt: (b, 0, 0)),
        scratch_shapes=[pltpu.VMEM((8, ATT_W), F32), pltpu.VMEM((8, 1), F32)],
    )
    return pl.pallas_call(
        kern,
        grid_spec=grid_spec,
        out_shape=jax.ShapeDtypeStruct((nb, 1, ATT_W), F32),
        compiler_params=_cparams("parallel", "arbitrary"),
        name="sb_decode",
    )(page_table, q3, g_row, *([cache_k] * pps), *([cache_v] * pps))


def _fox_decode_kernel(pt_ref, q_ref, k_ref, v_ref, f_ref, gq_ref, gk_ref, fb_ref, g_ref, *rest, pps):
    k_refs, v_refs, f_refs = rest[:pps], rest[pps:2 * pps], rest[2 * pps:3 * pps]
    o_ref, kn_ref, lf_ref, acc_ref, m_ref, l_ref, r_ref, qb_ref = rest[3 * pps:]
    c = pl.program_id(1)
    own = _head_rows(ATT_W)

    @pl.when(c == 0)
    def _():
        def head_norm(x_row, gain_row):
            xb = jnp.where(own, x_row, 0.0)
            ms = jnp.sum(xb * xb, axis=1, keepdims=True) * (1.0 / HEAD_DIM)
            return xb * lax.rsqrt(ms + EPS) * gain_row

        qn = head_norm(q_ref[0], gq_ref[...]) * (HEAD_DIM ** -0.5)
        kn = head_norm(k_ref[0], gk_ref[...])
        kn_ref[0] = jnp.sum(kn, axis=0, keepdims=True)
        fx = f_ref[0] + fb_ref[...]
        lf = jnp.minimum(fx, 0.0) - jnp.log(1.0 + jnp.exp(-jnp.abs(fx)))
        lf_ref[0] = lf
        qb_ref[...] = qn
        m_ref[...] = jnp.sum(qn * kn, axis=1, keepdims=True)
        l_ref[...] = jnp.ones_like(l_ref)
        acc_ref[...] = jnp.broadcast_to(v_ref[0], acc_ref.shape)
        r_ref[...] = _lane_to_rows(lf)

    qb = qb_ref[...]
    u = _tri(LANES, lambda r, cc: r >= cc)
    m_old, run = m_ref[...], r_ref[...]
    pad = jnp.zeros((8 - N_HEADS, LANES), F32)
    order = list(reversed(range(pps)))
    zs = [_dot3(qb, k_refs[i][0]) for i in order]
    lfs =[jnp.concatenate([f_refs[i][0], pad], axis=0) for i in order]
    css = [_dot_ones(lf, u) for lf in lfs]
    ss = []
    for z, lf, cs in zip(zs, lfs, css):
        ss.append(z + (cs - lf) + run)
        run = run + cs[:, 0:1]
    m_new = m_old
    for s in ss:
        m_new = jnp.maximum(m_new, jnp.max(s, axis=1, keepdims=True))
    alpha = jnp.exp(m_old - m_new)
    l = alpha * l_ref[...]
    acc = alpha * acc_ref[...]
    for i, s in zip(order, ss):
        pr = jnp.exp(s - m_new)
        l = l + jnp.sum(pr, axis=1, keepdims=True)
        acc = acc + _dot3_nt(pr, v_refs[i][0])
    acc_ref[...], m_ref[...], l_ref[...], r_ref[...] = acc, m_new, l, run

    @pl.when(c == pl.num_programs(1) - 1)
    def _():
        _finish_decode(acc / l, own, g_ref, o_ref)


def fox_decode(q3, k3, v3, f3, gq_row, gk_row, fb_row, g_row, cache_k, cache_v, cache_ft, page_table,
               *, page_off, pps):
    nb = q3.shape[0]
    n_pages = page_table.shape[1]
    n_chunks = n_pages // pps
    page = cache_k.shape[2]
    kern = functools.partial(_fox_decode_kernel, pps=pps)
    pages = _page_specs(n_chunks, pps, page_off, (1, ATT_W, page))
    f_pages = _page_specs(n_chunks, pps, page_off, (1, N_HEADS, page))
    tok = lambda w: pl.BlockSpec((1, 1, w), lambda b, c, pt: (b, 0, 0))
    par = lambda w: pl.BlockSpec((1, w), lambda b, c, pt: (0, 0))
    grid_spec = pltpu.PrefetchScalarGridSpec(
        num_scalar_prefetch=1,
        grid=(nb, n_chunks),
        in_specs=[tok(ATT_W), tok(ATT_W), tok(ATT_W), tok(LANES),
                  par(ATT_W), par(ATT_W), par(LANES), par(ATT_W)] + pages + pages + f_pages,
        out_specs=[tok(ATT_W), tok(ATT_W), tok(LANES)],
        scratch_shapes=[pltpu.VMEM((8, ATT_W), F32), pltpu.VMEM((8, 1), F32), pltpu.VMEM((8, 1), F32),
                        pltpu.VMEM((8, 1), F32), pltpu.VMEM((8, ATT_W), F32)],
    )
    return pl.pallas_call(
        kern,
        grid_spec=grid_spec,
        out_shape=[jax.ShapeDtypeStruct((nb, 1, ATT_W), F32),
                   jax.ShapeDtypeStruct((nb, 1, ATT_W), F32),
                   jax.ShapeDtypeStruct((nb, 1, LANES), F32)],
        compiler_params=_cparams("parallel", "arbitrary"),
        name="fox_decode",
    )(page_table, q3, k3, v3, f3, gq_row, gk_row, fb_row, g_row,
      *([cache_k] * pps), *([cache_v] * pps), *([cache_ft] * pps))


def _row_to_col(row_vec):
    r = lax.broadcasted_iota(jnp.int32, (HG_D, HG_D), 0)
    c = lax.broadcasted_iota(jnp.int32, (HG_D, HG_D), 1)
    return jnp.sum(jnp.where(r == c, row_vec, 0.0), axis=1, keepdims=True)


def _hgrn_decode_kernel(hq_ref, hf_ref, hi_ref, hg_ref, lb_ref, g_ref, s_ref, o_ref, sn_ref):
    outs = []
    for h in range(N_HEADS):
        ws = slice(h * HG_D, (h + 1) * HG_D)
        q, f, k = _hgrn_gates(hq_ref[0][:, ws], hf_ref[0][:, ws], lb_ref[:, ws])
        s_new = s_ref[0, h] * _row_to_col(f) + _row_to_col(k) * hi_ref[0][:, ws]
        sn_ref[0, h] = s_new
        o = jnp.sum(_row_to_col(q) * s_new, axis=0, keepdims=True)
        outs.append(_rms(o, g_ref[...]) * _sigmoid(hg_ref[0][:, ws]))
    o_ref[0] = jnp.concatenate(outs, axis=1)


def hgrn_decode(hq3, hf3, hi3, hg3, lb_row, g_row, state):
    nb = hq3.shape[0]
    tok = pl.BlockSpec((1, 1, HG_W), lambda b: (b, 0, 0))
    st = pl.BlockSpec((1, N_HEADS, HG_D, HG_D), lambda b: (b, 0, 0, 0))
    return pl.pallas_call(
        _hgrn_decode_kernel,
        grid=(nb,),
        in_specs=[tok, tok, tok, tok,
                  pl.BlockSpec((1, HG_W), lambda b: (0, 0)),
                  pl.BlockSpec((1, HG_D), lambda b: (0, 0)), st],
        out_specs=[tok, st],
        out_shape=[jax.ShapeDtypeStruct((nb, 1, HG_W), F32),
                   jax.ShapeDtypeStruct(state.shape, F32)],
        compiler_params=_cparams("parallel"),
        name="hgrn_decode",
    )(hq3, hf3, hi3, hg3, lb_row, g_row, state)


def _arrange_w_in(w_in_l):
    d = w_in_l.shape[0]
    a = 2 * 3 * ATT_W
    main = jnp.concatenate([w_in_l[:, :a], w_in_l[:, a + N_HEADS:]], axis=1)
    fcols = w_in_l[:, a:a + N_HEADS]
    pad = jnp.zeros((d, N_IN_PAD - C_FXF - N_HEADS), w_in_l.dtype)
    return jnp.concatenate([main, fcols, pad], axis=1).astype(F32)


def _seg_matrix():
    r = lax.broadcasted_iota(jnp.int32, (ATT_W, ATT_W), 0) // HEAD_DIM
    c = lax.broadcasted_iota(jnp.int32, (ATT_W, ATT_W), 1) // HEAD_DIM
    return jnp.where(r == c, 1.0, 0.0).astype(BF16)


def _row(v, width=None):
    v = v.reshape(1, -1).astype(F32)
    if width is not None and v.shape[1] < width:
        v = jnp.pad(v, ((0, 0), (0, width - v.shape[1])))
    return v


def kernel(x_prompt, x_sample, cache_sb_k, cache_sb_v, cache_fox_k, cache_fox_v, cache_fox_logf, state_hgrn,
           page_table, w_in, w_out, norm_mix_g, norm_ffn_g, fox_q_norm_g, fox_k_norm_g, fox_f_bias, sb_out_g,
           fox_out_g, hgrn_out_g, hgrn_lb_logits, ffn_w_gate, ffn_w_up, ffn_w_down, moe_router_w,
           moe_router_b, moe_w_gate, moe_w_up, moe_w_down):
    batch, seq, d_model = x_prompt.shape
    dec_b = x_sample.shape[0]
    depth = w_in.shape[0]
    n_pool, page = cache_sb_k.shape[1], cache_sb_k.shape[2]
    n_experts = moe_router_w.shape[-1]

    lb_all = lower_bounds(hgrn_lb_logits.astype(F32))
    seg = _seg_matrix()
    def pages_t(c):
        return jnp.transpose(c, (0, 1, 3, 4, 2)).reshape(depth * n_pool, ATT_W, page)

    sbk_c, sbv_c, fxk_c, fxv_c = (pages_t(c) for c in (cache_sb_k, cache_sb_v, cache_fox_k, cache_fox_v))
    fxf_c = jnp.swapaxes(cache_fox_logf, 2, 3).reshape(depth * n_pool, N_HEADS, page)

    xp = x_prompt.reshape(batch * seq, d_model)
    xs = x_sample.reshape(dec_b, d_model)
    p_out = [[] for _ in range(6)]
    s_out = [[] for _ in range(6)]

    for l in range(depth):
        w_in_f = _arrange_w_in(w_in[l])
        w_in_l = w_in_f.astype(BF16)
        w_out_f = w_out[l].astype(F32)
        w_out_l = w_out_f.astype(BF16)
        g_mix, g_ffn = _row(norm_mix_g[l]), _row(norm_ffn_g[l])
        gq = _row(jnp.tile(fox_q_norm_g[l], N_HEADS))
        gk = _row(jnp.tile(fox_k_norm_g[l], N_HEADS))
        fb = _row(fox_f_bias[l], LANES)
        g_sb, g_fx, g_hg = _row(sb_out_g[l]), _row(fox_out_g[l]), _row(hgrn_out_g[l])
        lb_row = lb_all[l:l + 1]

        p = in_proj(xp, g_mix, w_in_l, tm=512, tn=1280)
        fxk_n, lf, qa, ka = fox_prep(p, gq, gk, fb, seg, batch=batch, seq=seq, tb=256)
        sb_o = sb_attention(p, g_sb, batch=batch, seq=seq, tq=512, tk=256)
        fx_o = fox_attention(qa, ka, p, g_fx, batch=batch, seq=seq, tq=256)
        hg_o, s_fin = hgrn_prompt(p, lb_row, g_hg, batch=batch, seq=seq, chunk=128, sub=16)
        hp = out_proj(xp, sb_o, fx_o, hg_o, w_out_l, tm=512)
        p_out[0].append(p[:, C_SBK:C_SBK + ATT_W])
        p_out[1].append(p[:, C_SBV:C_SBV + ATT_W])
        p_out[2].append(fxk_n)
        p_out[3].append(p[:, C_FXV:C_FXV + ATT_W])
        p_out[4].append(lf)
        p_out[5].append(s_fin)

        ps = in_proj(xs, g_mix, w_in_f, tm=dec_b, tn=1280)
        tok = lambda c0, w: ps[:, c0:c0 + w].reshape(dec_b, 1, w)
        off = l * n_pool
        sb_os = sb_decode(tok(C_SBQ, ATT_W), g_sb, sbk_c, sbv_c, page_table, page_off=off, pps=16)
        fx_os, fxk_s, lf_s = fox_decode(tok(C_FXQ, ATT_W), tok(C_FXK, ATT_W), tok(C_FXV, ATT_W),
                                        tok(C_FXF, LANES), gq, gk, fb, g_fx, fxk_c, fxv_c, fxf_c,
                                        page_table, page_off=off, pps=16)
        hg_os, s_new = hgrn_decode(tok(C_HGQ, HG_W), tok(C_HGF, HG_W), tok(C_HGI, HG_W), tok(C_HGG, HG_W),
                                   lb_row, g_hg, state_hgrn[l].astype(F32))
        hs = out_proj(xs, sb_os.reshape(dec_b, ATT_W), fx_os.reshape(dec_b, ATT_W),
                      hg_os.reshape(dec_b, HG_W), w_out_f, tm=dec_b)
        s_out[0].append(ps[:, C_SBK:C_SBK + ATT_W])
        s_out[1].append(ps[:, C_SBV:C_SBV + ATT_W])
        s_out[2].append(fxk_s.reshape(dec_b, ATT_W))
        s_out[3].append(ps[:, C_FXV:C_FXV + ATT_W])
        s_out[4].append(lf_s.reshape(dec_b, LANES)[:, :N_HEADS])
        s_out[5].append(s_new)

        if l % 2 == 0:
            wf = [w[l // 2].astype(F32) for w in (ffn_w_gate, ffn_w_up, ffn_w_down)]
            wg, wu, wd = (w.astype(BF16) for w in wf)
            xp = ffn_dense(hp, g_ffn, wg, wu, wd, tm=1024, tf=256)
            xs = ffn_dense(hs, g_ffn, *wf, tm=dec_b, tf=256)
        else:
            wf = [w[l // 2].astype(F32) for w in (moe_w_gate, moe_w_up, moe_w_down)]
            wg, wu, wd = (w.astype(BF16) for w in wf)
            rw = jnp.pad(moe_router_w[l // 2].astype(F32), ((0, 0), (0, LANES - n_experts)))
            rb = _row(moe_router_b[l // 2], LANES)
            gates_p = moe_router(hp, g_ffn, rw, rb, tm=512, n_experts=n_experts)
            xp = moe_ffn(hp, g_ffn, gates_p, wg, wu, wd, tm=512)
            gates_s = moe_router(hs, g_ffn, rw, rb, tm=dec_b, n_experts=n_experts)
            xs = moe_ffn(hs, g_ffn, gates_s, *wf, tm=dec_b)

    def stack(parts, shape):
        return jnp.stack(parts).reshape((depth,) + shape)

    hd = (N_HEADS, HEAD_DIM)
    y_prompt = xp.reshape(batch, seq, d_model)
    y_sample = xs.reshape(dec_b, 1, d_model)
    outs_p = [stack(p_out[i], (batch, seq) + hd) for i in range(4)]
    outs_p.append(stack(p_out[4], (batch, seq, N_HEADS)))
    outs_p.append(stack(p_out[5], (batch, N_HEADS, HG_D, HG_D)).astype(state_hgrn.dtype))
    outs_s = [stack(s_out[i], (dec_b, 1) + hd) for i in range(4)]
    outs_s.append(stack(s_out[4], (dec_b, 1, N_HEADS)))
    outs_s.append(stack(s_out[5], (dec_b, N_HEADS, HG_D, HG_D)).astype(state_hgrn.dtype))
    return (y_prompt, y_sample, *outs_p, *outs_s)
```

```python
import functools

import jax
import jax.numpy as jnp
from jax import lax
from jax.experimental import pallas as pl
from jax.experimental.pallas import tpu as pltpu

F32 = jnp.float32
BF16 = jnp.bfloat16
EPS = 1e-6
NEG_BIG = -1e30

HEAD_DIM = 64
N_HEADS = 4
ATT_W = N_HEADS * HEAD_DIM
HG_D = 128
HG_W = N_HEADS * HG_D
LANES = 128
VMEM_LIMIT = 56 * 1024 * 1024

C_SBQ, C_SBK, C_SBV = 0, 256, 512
C_FXQ, C_FXK, C_FXV = 768, 1024, 1280
C_HGQ, C_HGF, C_HGI, C_HGG = 1536, 2048, 2560, 3072
C_FXF = 3584
N_IN_PAD = 3840


def _cparams(*sem):
    return pltpu.CompilerParams(dimension_semantics=sem, vmem_limit_bytes=VMEM_LIMIT)


def _rms(xf, g_row):
    ms = jnp.mean(xf * xf, axis=-1, keepdims=True)
    return xf * lax.rsqrt(ms + EPS) * g_row


def _softplus(z):
    return jnp.maximum(z, 0.0) + jnp.log(1.0 + jnp.exp(-jnp.abs(z)))


def _sigmoid(x):
    return 1.0 / (1.0 + jnp.exp(-x))


def _split2(x):
    hi = x.astype(BF16)
    lo = (x - hi.astype(F32)).astype(BF16)
    return hi, lo


def _split3(x):
    hi = x.astype(BF16)
    r = x - hi.astype(F32)
    mid = r.astype(BF16)
    lo = (r - mid.astype(F32)).astype(BF16)
    return hi, mid, lo


def _dot(a, b):
    return jnp.dot(a, b, preferred_element_type=F32)


def _dot_nt(a, b):
    return lax.dot_general(a, b, (((1,), (1,)), ((), ())), preferred_element_type=F32)


def _stack_split2(x):
    hi, lo = _split2(x)
    return hi, jnp.concatenate([hi, lo], axis=0)


def _dot3(x, w):
    m = x.shape[0]
    xh, x2 = _stack_split2(x)
    wh, wl = _split2(w)
    r = _dot(x2, wh)
    return r[:m] + r[m:] + _dot(xh, wl)


def _dot3_nt(x, w):
    m = x.shape[0]
    xh, x2 = _stack_split2(x)
    wh, wl = _split2(w)
    r = _dot_nt(x2, wh)
    return r[:m] + r[m:] + _dot_nt(xh, wl)


def _dot_ones(x, ones_bf16):
    m = x.shape[0]
    r = _dot(jnp.concatenate(_split3(x), axis=0), ones_bf16)
    return r[:m] + r[m:2 * m] + r[2 * m:]


def _mm(x, w, precise):
    return _dot3(x, w) if precise else _dot(x.astype(BF16), w)


def _tri(n, rel):
    r = lax.broadcasted_iota(jnp.int32, (n, n), 0)
    c = lax.broadcasted_iota(jnp.int32, (n, n), 1)
    return jnp.where(rel(r, c), 1.0, 0.0).astype(BF16)


def _lower_bounds_kernel(x_ref, o_ref):
    x = x_ref[...]
    depth = x.shape[0]
    m = jnp.max(x, axis=0, keepdims=True)
    e = jnp.exp(x - m)
    sm = e / jnp.sum(e, axis=0, keepdims=True)
    run = jnp.zeros_like(sm[0:1])
    rows = []
    for l in range(depth):
        run = run + sm[l:l + 1]
        rows.append(run - sm[0:1])
    o_ref[...] = jnp.concatenate(rows, axis=0)


def lower_bounds(logits):
    return pl.pallas_call(
        _lower_bounds_kernel,
        out_shape=jax.ShapeDtypeStruct(logits.shape, F32),
        name="hgrn_lower_bounds",
    )(logits)


def _in_proj_kernel(x_ref, g_ref, w_ref, o_ref, xn_ref, *, precise):
    @pl.when(pl.program_id(1) == 0)
    def _():
        xn_ref[...] = _rms(x_ref[...], g_ref[...]).astype(xn_ref.dtype)

    o_ref[...] = _mm(xn_ref[...], w_ref[...], precise)


def in_proj(x, g_row, w, *, tm, tn):
    n, d = x.shape
    n_out = w.shape[1]
    precise = w.dtype == F32
    return pl.pallas_call(
        functools.partial(_in_proj_kernel, precise=precise),
        grid=(n // tm, n_out // tn),
        in_specs=[pl.BlockSpec((tm, d), lambda i, j: (i, 0)),
                  pl.BlockSpec((1, d), lambda i, j: (0, 0)),
                  pl.BlockSpec((d, tn), lambda i, j: (0, j))],
        out_specs=pl.BlockSpec((tm, tn), lambda i, j: (i, j)),
        out_shape=jax.ShapeDtypeStruct((n, n_out), F32),
        scratch_shapes=[pltpu.VMEM((tm, d), F32 if precise else BF16)],
        compiler_params=_cparams("parallel", "arbitrary"),
        name="norm_in_proj",
    )(x, g_row, w)


STACKED_COLS = (C_SBK, C_SBV, C_FXV)


def _in_proj_stack_kernel(x_ref, g_ref, w_ref, *rest, tn, n_alias):
    o_ref, s_refs, xn_ref = rest[n_alias], rest[n_alias + 1:-1], rest[-1]
    j = pl.program_id(1)

    @pl.when(j == 0)
    def _():
        xn_ref[...] = _rms(x_ref[...], g_ref[...]).astype(BF16)

    res = _dot(xn_ref[...], w_ref[...])
    o_ref[...] = res
    for c0, s_ref in zip(STACKED_COLS, s_refs):
        @pl.when(j == c0 // tn)
        def _(c0=c0, s_ref=s_ref):
            s_ref[0] = res[:, c0 % tn:c0 % tn + ATT_W]


def in_proj_stacked(x, g_row, w_bf16, stacks, layer, depth, *, tm, tn):
    n, d = x.shape
    n_out = w_bf16.shape[1]
    assert all(c0 % tn + ATT_W <= tn for c0 in STACKED_COLS)
    n_alias = 0 if stacks is None else len(STACKED_COLS)
    stack_sds = jax.ShapeDtypeStruct((depth, n, ATT_W), F32)
    outs = pl.pallas_call(
        functools.partial(_in_proj_stack_kernel, tn=tn, n_alias=n_alias),
        grid=(n // tm, n_out // tn),
        in_specs=[pl.BlockSpec((tm, d), lambda i, j: (i, 0)),
                  pl.BlockSpec((1, d), lambda i, j: (0, 0)),
                  pl.BlockSpec((d, tn), lambda i, j: (0, j))]
                 + [pl.BlockSpec(memory_space=pl.ANY)] * n_alias,
        out_specs=[pl.BlockSpec((tm, tn), lambda i, j: (i, j))]
                  + [pl.BlockSpec((1, tm, ATT_W), lambda i, j: (layer, i, 0))] * len(STACKED_COLS),
        out_shape=[jax.ShapeDtypeStruct((n, n_out), F32)] + [stack_sds] * len(STACKED_COLS),
        input_output_aliases={3 + a: 1 + a for a in range(n_alias)},
        scratch_shapes=[pltpu.VMEM((tm, d), BF16)],
        compiler_params=_cparams("parallel", "arbitrary"),
        name="norm_in_proj_stacked",
    )(x, g_row, w_bf16, *(stacks or ()))
    return outs[0], tuple(outs[1:])


def _seg_mean_sq(x, seg_ref):
    hi, lo = _split2(x * x)
    return (_dot(hi, seg_ref[...]) + _dot(lo, seg_ref[...])) * (1.0 / HEAD_DIM)


def _fox_prep_kernel(q_ref, k_ref, f_ref, gq_ref, gk_ref, fb_ref, seg_ref, *rest):
    kn_ref, lf_ref, qa_ref, ka_ref, carry_ref = rest[-5:]
    tb = q_ref.shape[0]

    @pl.when(pl.program_id(1) == 0)
    def _():
        carry_ref[...] = jnp.zeros_like(carry_ref)

    xq = q_ref[...]
    xk = k_ref[...]
    qn = xq * lax.rsqrt(_seg_mean_sq(xq, seg_ref) + EPS) * gq_ref[...]
    kn = xk * lax.rsqrt(_seg_mean_sq(xk, seg_ref) + EPS) * gk_ref[...]
    kn_ref[0] = kn

    fx = f_ref[...] + fb_ref[...]
    lf = jnp.minimum(fx, 0.0) - jnp.log(1.0 + jnp.exp(-jnp.abs(fx)))
    lf_ref[...] = lf[:, :N_HEADS]

    ltri = _tri(tb, lambda r, c: c <= r)
    p0, p1, p2 = _split3(lf)
    cum = _dot(ltri, p0) + _dot(ltri, p1) + _dot(ltri, p2) + carry_ref[...]
    carry_ref[...] = cum[tb - 1:tb, :]

    c0, c1, c2 = (c.astype(F32) for c in _split3(cum))
    lane = lax.broadcasted_iota(jnp.int32, (tb, HEAD_DIM), 1)
    one = jnp.where(lane < 6, 1.0, 0.0)
    qs = qn * (HEAD_DIM ** -0.5)
    q_parts, k_parts = [], []
    for h in range(N_HEADS):
        a0, a1, a2 = c0[:, h:h + 1], c1[:, h:h + 1], c2[:, h:h + 1]
        ext_q = jnp.where(lane == 0, a0, jnp.where(lane == 1, a1, jnp.where(lane == 2, a2, one)))
        ext_k = jnp.where(lane == 3, -a0, jnp.where(lane == 4, -a1, jnp.where(lane == 5, -a2, one)))
        q_parts += [qs[:, h * HEAD_DIM:(h + 1) * HEAD_DIM], ext_q]
        k_parts += [kn[:, h * HEAD_DIM:(h + 1) * HEAD_DIM], ext_k]
    qa_ref[...] = jnp.concatenate(q_parts, axis=1).astype(BF16)
    ka_ref[...] = jnp.concatenate(k_parts, axis=1).astype(BF16)


def fox_prep(p, gq_row, gk_row, fb_row, seg, kn_stack, layer, depth, *, batch, seq, tb):
    n = p.shape[0]
    nb = seq // tb
    row = lambda b, t: b * nb + t
    aw = N_HEADS * LANES
    alias = () if kn_stack is None else (kn_stack,)
    return pl.pallas_call(
        _fox_prep_kernel,
        grid=(batch, nb),
        in_specs=[pl.BlockSpec((tb, ATT_W), lambda b, t: (row(b, t), C_FXQ // ATT_W)),
                  pl.BlockSpec((tb, ATT_W), lambda b, t: (row(b, t), C_FXK // ATT_W)),
                  pl.BlockSpec((tb, LANES), lambda b, t: (row(b, t), C_FXF // LANES)),
                  pl.BlockSpec((1, ATT_W), lambda b, t: (0, 0)),
                  pl.BlockSpec((1, ATT_W), lambda b, t: (0, 0)),
                  pl.BlockSpec((1, LANES), lambda b, t: (0, 0)),
                  pl.BlockSpec((ATT_W, ATT_W), lambda b, t: (0, 0))]
                 + [pl.BlockSpec(memory_space=pl.ANY)] * len(alias),
        out_specs=[pl.BlockSpec((1, tb, ATT_W), lambda b, t: (layer, row(b, t), 0)),
                   pl.BlockSpec((tb, N_HEADS), lambda b, t: (row(b, t), 0)),
                   pl.BlockSpec((tb, aw), lambda b, t: (row(b, t), 0)),
                   pl.BlockSpec((tb, aw), lambda b, t: (row(b, t), 0))],
        out_shape=[jax.ShapeDtypeStruct((depth, n, ATT_W), F32),
                   jax.ShapeDtypeStruct((n, N_HEADS), F32),
                   jax.ShapeDtypeStruct((n, aw), BF16),
                   jax.ShapeDtypeStruct((n, aw), BF16)],
        input_output_aliases={7: 0} if alias else {},
        scratch_shapes=[pltpu.VMEM((1, LANES), F32)],
        compiler_params=_cparams("parallel", "arbitrary"),
        name="fox_prep",
    )(p, p, p, gq_row, gk_row, fb_row, seg, *alias)


def _store_v_transposed(v_ref, vt_ref, tk):
    for j in range(vt_ref.shape[0]):
        vt_ref[j] = v_ref[j * tk:(j + 1) * tk, :].T.astype(BF16)


def _finish_attention(ot, g_ref, o_ref):
    ms = jnp.mean(ot * ot, axis=0, keepdims=True)
    o_ref[...] = (ot * lax.rsqrt(ms + EPS)).T * g_ref[...]


def _sb_attn_kernel(q_ref, k_ref, v_ref, g_ref, o_ref, acc_ref, r_ref, *, tk):
    i = pl.program_id(1)
    tq = q_ref.shape[0]
    acc_ref[...] = jnp.zeros_like(acc_ref)
    r_ref[...] = jnp.zeros_like(r_ref)
    u = _tri(tk, lambda r, c: r >= c)
    n_diag = tq // tk
    qry_i = lax.broadcasted_iota(jnp.int32, (tq, tk), 0)
    key_i = lax.broadcasted_iota(jnp.int32, (tq, tk), 1)
    low_half = lax.broadcasted_iota(jnp.int32, (tq, LANES), 1) < HEAD_DIM
    qm = []
    for pair in range(N_HEADS // 2):
        qp = q_ref[:, pair * LANES:(pair + 1) * LANES] * (HEAD_DIM ** -0.5)
        qm.append(jnp.where(low_half, qp, 0.0).astype(BF16))
        qm.append(jnp.where(low_half, 0.0, qp).astype(BF16))

    def block(j, diag_off):
        row0 = pl.multiple_of(j * tk, tk)
        before = None if diag_off is None else (key_i + diag_off < qry_i)
        for pair in range(N_HEADS // 2):
            ls = slice(pair * LANES, (pair + 1) * LANES)
            kp = k_ref[pl.ds(row0, tk), ls].astype(BF16)
            vp = v_ref[pl.ds(row0, tk), ls].astype(BF16)
            res = []
            for h in (2 * pair, 2 * pair + 1):
                z = _dot_nt(qm[h], kp)
                sp = _softplus(z)
                if before is not None:
                    sp = jnp.where(before, sp, 0.0)
                cs = _dot(sp.astype(BF16), u)
                r_old = r_ref[h]
                a = jnp.exp(z - cs - r_old)
                if before is not None:
                    a = jnp.where(before, a, 0.0)
                res.append(_dot(a.astype(BF16), vp))
                r_ref[h] = r_old + cs[:, 0:1]
            acc_ref[:, ls] += jnp.where(low_half, res[0], res[1])

    for d in reversed(range(n_diag)):
        block(i * n_diag + d, d * tk)

    def body(jj, c):
        block(i * n_diag - 1 - jj, None)
        return c

    lax.fori_loop(0, i * n_diag, body, 0)
    o_ref[...] = _rms(acc_ref[...], g_ref[...])


def sb_attention(p, g_row, *, batch, seq, tq, tk):
    n = p.shape[0]
    nq = seq // tq
    assert tq % tk == 0
    kern = functools.partial(_sb_attn_kernel, tk=tk)
    return pl.pallas_call(
        kern,
        grid=(batch, nq),
        in_specs=[pl.BlockSpec((tq, ATT_W), lambda b, i: (b * nq + i, C_SBQ // ATT_W)),
                  pl.BlockSpec((seq, ATT_W), lambda b, i: (b, C_SBK // ATT_W)),
                  pl.BlockSpec((seq, ATT_W), lambda b, i: (b, C_SBV // ATT_W)),
                  pl.BlockSpec((1, ATT_W), lambda b, i: (0, 0))],
        out_specs=pl.BlockSpec((tq, ATT_W), lambda b, i: (b * nq + i, 0)),
        out_shape=jax.ShapeDtypeStruct((n, ATT_W), F32),
        scratch_shapes=[pltpu.VMEM((tq, ATT_W), F32),
                        pltpu.VMEM((N_HEADS, tq, 1), F32)],
        compiler_params=_cparams("parallel", "arbitrary"),
        name="sb_attention",
    )(p, p, p, g_row)


def _fox_attn_kernel(q_ref, k_ref, v_ref, g_ref, o_ref, vt_ref, acc_ref, m_ref, l_ref, *, tk):
    i = pl.program_id(1)
    tq = q_ref.shape[0]

    @pl.when(i == 0)
    def _():
        _store_v_transposed(v_ref, vt_ref, tk)

    acc_ref[...] = jnp.zeros_like(acc_ref)
    m_ref[...] = jnp.full_like(m_ref, NEG_BIG)
    l_ref[...] = jnp.zeros_like(l_ref)
    key_i = lax.broadcasted_iota(jnp.int32, (tk, tq), 0)
    qry_i = lax.broadcasted_iota(jnp.int32, (tk, tq), 1)
    causal = key_i <= qry_i

    def block(j, diagonal):
        row0 = pl.multiple_of(j * tk, tk)
        sts = [_dot_nt(k_ref[pl.ds(row0, tk), h * LANES:(h + 1) * LANES],
                       q_ref[:, h * LANES:(h + 1) * LANES]) for h in range(N_HEADS)]
        for h in range(N_HEADS):
            hs = slice(h * HEAD_DIM, (h + 1) * HEAD_DIM)
            st = sts[h]
            if diagonal:
                st = jnp.where(causal, st, NEG_BIG)
            m_old = m_ref[h:h + 1, :]
            m_new = jnp.maximum(m_old, jnp.max(st, axis=0, keepdims=True))
            alpha = jnp.exp(m_old - m_new)
            pt = jnp.exp(st - m_new)
            l_ref[h:h + 1, :] = alpha * l_ref[h:h + 1, :] + jnp.sum(pt, axis=0, keepdims=True)
            acc_ref[hs, :] = alpha * acc_ref[hs, :] + _dot(vt_ref[j, hs, :], pt.astype(BF16))
            m_ref[h:h + 1, :] = m_new

    block(i, True)

    def body(jj, c):
        block(i - 1 - jj, False)
        return c

    lax.fori_loop(0, i, body, 0)
    parts = [acc_ref[h * HEAD_DIM:(h + 1) * HEAD_DIM, :] / l_ref[h:h + 1, :] for h in range(N_HEADS)]
    _finish_attention(jnp.concatenate(parts, axis=0), g_ref, o_ref)


def fox_attention(qa, ka, p, g_row, *, batch, seq, tq):
    n = p.shape[0]
    nq = seq // tq
    aw = N_HEADS * LANES
    kern = functools.partial(_fox_attn_kernel, tk=tq)
    return pl.pallas_call(
        kern,
        grid=(batch, nq),
        in_specs=[pl.BlockSpec((tq, aw), lambda b, i: (b * nq + i, 0)),
                  pl.BlockSpec((seq, aw), lambda b, i: (b, 0)),
                  pl.BlockSpec((seq, ATT_W), lambda b, i: (b, C_FXV // ATT_W)),
                  pl.BlockSpec((1, ATT_W), lambda b, i: (0, 0))],
        out_specs=pl.BlockSpec((tq, ATT_W), lambda b, i: (b * nq + i, 0)),
        out_shape=jax.ShapeDtypeStruct((n, ATT_W), F32),
        scratch_shapes=[pltpu.VMEM((nq, ATT_W, tq), BF16),
                        pltpu.VMEM((ATT_W, tq), F32),
                        pltpu.VMEM((8, tq), F32),
                        pltpu.VMEM((8, tq), F32)],
        compiler_params=_cparams("parallel", "arbitrary"),
        name="fox_attention",
    )(qa, ka, p, g_row)


def _hgrn_gates(hq, hf, lb):
    q = hq * _sigmoid(hq)
    f = lb + (1.0 - lb) * _sigmoid(hf)
    return q, f, 1.0 - f


def _hgrn_kernel(hq_ref, hf_ref, hi_ref, hg_ref, lb_ref, g_ref, o_ref, s_ref, st_ref, *, sub):
    c = hq_ref.shape[0]
    n_sub = c // sub

    @pl.when(pl.program_id(1) == 0)
    def _():
        st_ref[...] = jnp.zeros_like(st_ref)

    ltri = _tri(c, lambda r, cc: cc <= r)
    t_i = lax.broadcasted_iota(jnp.int32, (c, c), 0)
    s_i = lax.broadcasted_iota(jnp.int32, (c, c), 1)
    pos = lax.broadcasted_iota(jnp.int32, (c, HG_D), 0)
    pos_sub = lax.broadcasted_iota(jnp.int32, (n_sub, sub, 1), 1)

    q_all, f_all, k_all = _hgrn_gates(hq_ref[...], hf_ref[...], lb_ref[...])
    p0, p1, p2 = _split3(jnp.log(f_all))
    g3 = _dot(ltri, jnp.concatenate([p0, p1, p2], axis=1))
    g_all = g3[:, :HG_W] + g3[:, HG_W:2 * HG_W] + g3[:, 2 * HG_W:]
    gate_all = _sigmoid(hg_ref[...])

    outs = []
    for h in range(N_HEADS):
        ws = slice(h * HG_D, (h + 1) * HG_D)
        q, k, g, v = q_all[:, ws], k_all[:, ws], g_all[:, ws], hi_ref[:, ws]
        st = st_ref[h]
        o = _dot_nt((q * jnp.exp(g)).astype(BF16), st.astype(BF16))

        q3, k3, g3d, v3 = (a.reshape(n_sub, sub, HG_D) for a in (q, k, g, v))
        o_diag = jnp.zeros((n_sub, sub, HG_D), F32)
        for s in range(sub):
            dec = jnp.exp(jnp.minimum(g3d - g3d[:, s:s + 1, :], 0.0))
            col = jnp.sum(q3 * k3[:, s:s + 1, :] * dec, axis=-1, keepdims=True)
            o_diag = o_diag + jnp.where(pos_sub >= s, col, 0.0) * v3[:, s:s + 1, :]
        o = o + o_diag.reshape(c, HG_D)

        a = jnp.zeros((c, c), F32)
        blk = 2 * sub
        while blk <= c:
            half = blk // 2
            gr = g.reshape(c // blk, blk, HG_D)[:, half - 1:half, :]
            e = jnp.exp(-jnp.abs(g.reshape(c // blk, blk, HG_D) - gr)).reshape(c, HG_D)
            late = (pos & (blk - 1)) >= half
            qs = jnp.where(late, q * e, 0.0).astype(BF16)
            ks = jnp.where(late, 0.0, k * e).astype(BF16)
            a_l = _dot_nt(qs, ks)
            sh = blk.bit_length() - 1
            a = a + (a_l if blk == c else jnp.where((t_i >> sh) == (s_i >> sh), a_l, 0.0))
            blk *= 2
        o = o + _dot(a.astype(BF16), v.astype(BF16))

        g_end = g[c - 1:c, :]
        k_end = (k * jnp.exp(g_end - g)).astype(BF16)
        st_ref[h] = st * jnp.exp(g_end) + _dot(v.T.astype(BF16), k_end)
        outs.append(_rms(o, g_ref[...]) * gate_all[:, ws])
    o_ref[...] = jnp.concatenate(outs, axis=1)

    @pl.when(pl.program_id(1) == pl.num_programs(1) - 1)
    def _():
        for h in range(N_HEADS):
            s_ref[0, h] = st_ref[h].T


def hgrn_prompt(p, lb_row, g_row, *, batch, seq, chunk, sub):
    n = p.shape[0]
    nc = seq // chunk
    col = lambda c0: (lambda b, t: (b * nc + t, c0 // HG_W))
    kern = functools.partial(_hgrn_kernel, sub=sub)
    return pl.pallas_call(
        kern,
        grid=(batch, nc),
        in_specs=[pl.BlockSpec((chunk, HG_W), col(C_HGQ)),
                  pl.BlockSpec((chunk, HG_W), col(C_HGF)),
                  pl.BlockSpec((chunk, HG_W), col(C_HGI)),
                  pl.BlockSpec((chunk, HG_W), col(C_HGG)),
                  pl.BlockSpec((1, HG_W), lambda b, t: (0, 0)),
                  pl.BlockSpec((1, HG_D), lambda b, t: (0, 0))],
        out_specs=[pl.BlockSpec((chunk, HG_W), lambda b, t: (b * nc + t, 0)),
                   pl.BlockSpec((1, N_HEADS, HG_D, HG_D), lambda b, t: (b, 0, 0, 0))],
        out_shape=[jax.ShapeDtypeStruct((n, HG_W), F32),
                   jax.ShapeDtypeStruct((batch, N_HEADS, HG_D, HG_D), F32)],
        scratch_shapes=[pltpu.VMEM((N_HEADS, HG_D, HG_D), F32)],
        compiler_params=_cparams("parallel", "arbitrary"),
        name="hgrn_prompt",
    )(p, p, p, p, lb_row, g_row)


def _out_proj_kernel(x_ref, a_ref, b_ref, c_ref, w_ref, o_ref, *, precise):
    mix = jnp.concatenate([a_ref[...], b_ref[...], c_ref[...]], axis=1)
    o_ref[...] = x_ref[...] + _mm(mix, w_ref[...], precise)


def out_proj(x, sb_o, fx_o, hg_o, w, *, tm):
    n, d = x.shape
    row = lambda i: (i, 0)
    return pl.pallas_call(
        functools.partial(_out_proj_kernel, precise=w.dtype == F32),
        grid=(n // tm,),
        in_specs=[pl.BlockSpec((tm, d), row),
                  pl.BlockSpec((tm, ATT_W), row),
                  pl.BlockSpec((tm, ATT_W), row),
                  pl.BlockSpec((tm, HG_W), row),
                  pl.BlockSpec(w.shape, lambda i: (0, 0))],
        out_specs=pl.BlockSpec((tm, d), row),
        out_shape=jax.ShapeDtypeStruct((n, d), F32),
        compiler_params=_cparams("parallel"),
        name="out_proj",
    )(x, sb_o, fx_o, hg_o, w)


def _ffn_kernel(h_ref, g_ref, wg_ref, wu_ref, wd_ref, o_ref, hn_ref, acc_ref, *, precise):
    j = pl.program_id(1)

    @pl.when(j == 0)
    def _():
        hn_ref[...] = _rms(h_ref[...], g_ref[...]).astype(hn_ref.dtype)
        acc_ref[...] = jnp.zeros_like(acc_ref)

    hn = hn_ref[...]
    gate = _mm(hn, wg_ref[...], precise)
    up = _mm(hn, wu_ref[...], precise)
    acc_ref[...] += _mm(gate * _sigmoid(gate) * up, wd_ref[...], precise)

    @pl.when(j == pl.num_programs(1) - 1)
    def _():
        o_ref[...] = h_ref[...] + acc_ref[...]


def ffn_dense(h, g_row, wg, wu, wd, *, tm, tf):
    n, d = h.shape
    ff = wg.shape[1]
    precise = wg.dtype == F32
    return pl.pallas_call(
        functools.partial(_ffn_kernel, precise=precise),
        grid=(n // tm, ff // tf),
        in_specs=[pl.BlockSpec((tm, d), lambda i, j: (i, 0)),
                  pl.BlockSpec((1, d), lambda i, j: (0, 0)),
                  pl.BlockSpec((d, tf), lambda i, j: (0, j)),
                  pl.BlockSpec((d, tf), lambda i, j: (0, j)),
                  pl.BlockSpec((tf, d), lambda i, j: (j, 0))],
        out_specs=pl.BlockSpec((tm, d), lambda i, j: (i, 0)),
        out_shape=jax.ShapeDtypeStruct((n, d), F32),
        scratch_shapes=[pltpu.VMEM((tm, d), F32 if precise else BF16), pltpu.VMEM((tm, d), F32)],
        compiler_params=_cparams("parallel", "arbitrary"),
        name="ffn_dense",
    )(h, g_row, wg, wu, wd)


def _router_kernel(h_ref, g_ref, w_ref, b_ref, o_ref, *, n_experts):
    hn = _rms(h_ref[...], g_ref[...])
    xh, xl = _split2(hn)
    wh, wl = _split2(w_ref[...])
    logits = _dot(xh, wh) + _dot(xl, wh) + _dot(xh, wl) + b_ref[...]
    lane = lax.broadcasted_iota(jnp.int32, logits.shape, 1).astype(F32)
    logits = jnp.where(lane < n_experts, logits, NEG_BIG)
    m1 = jnp.max(logits, axis=1, keepdims=True)
    i1 = jnp.min(jnp.where(logits == m1, lane, float(LANES)), axis=1, keepdims=True)
    first = lane == i1
    rest = jnp.where(first, NEG_BIG, logits)
    m2 = jnp.max(rest, axis=1, keepdims=True)
    i2 = jnp.min(jnp.where(rest == m2, lane, float(LANES)), axis=1, keepdims=True)
    second = lane == i2
    e = jnp.exp(m2 - m1)
    w1 = 1.0 / (1.0 + e)
    o_ref[...] = jnp.where(first, w1, 0.0) + jnp.where(second, e * w1, 0.0)


def moe_router(h, g_row, w_pad, b_pad, *, tm, n_experts):
    n, d = h.shape
    kern = functools.partial(_router_kernel, n_experts=n_experts)
    return pl.pallas_call(
        kern,
        grid=(n // tm,),
        in_specs=[pl.BlockSpec((tm, d), lambda i: (i, 0)),
                  pl.BlockSpec((1, d), lambda i: (0, 0)),
                  pl.BlockSpec((d, LANES), lambda i: (0, 0)),
                  pl.BlockSpec((1, LANES), lambda i: (0, 0))],
        out_specs=pl.BlockSpec((tm, LANES), lambda i: (i, 0)),
        out_shape=jax.ShapeDtypeStruct((n, LANES), F32),
        compiler_params=_cparams("parallel"),
        name="moe_router",
    )(h, g_row, w_pad, b_pad)


def _moe_kernel(h_ref, g_ref, gates_ref, wg_ref, wu_ref, wd_ref, o_ref, hn_ref, acc_ref, *, precise):
    e = pl.program_id(1)

    @pl.when(e == 0)
    def _():
        hn_ref[...] = _rms(h_ref[...], g_ref[...]).astype(hn_ref.dtype)
        acc_ref[...] = jnp.zeros_like(acc_ref)

    gates = gates_ref[...]
    lane = lax.broadcasted_iota(jnp.int32, gates.shape, 1)
    ge = jnp.sum(jnp.where(lane == e, gates, 0.0), axis=1, keepdims=True)
    hn = hn_ref[...]
    gate = _mm(hn, wg_ref[0], precise)
    up = _mm(hn, wu_ref[0], precise)
    acc_ref[...] += _mm(gate * _sigmoid(gate) * up * ge, wd_ref[0], precise)

    @pl.when(e == pl.num_programs(1) - 1)
    def _():
        o_ref[...] = h_ref[...] + acc_ref[...]


def moe_ffn(h, g_row, gates, wg, wu, wd, *, tm):
    n, d = h.shape
    n_e, _, ffe = wg.shape
    precise = wg.dtype == F32
    return pl.pallas_call(
        functools.partial(_moe_kernel, precise=precise),
        grid=(n // tm, n_e),
        in_specs=[pl.BlockSpec((tm, d), lambda i, e: (i, 0)),
                  pl.BlockSpec((1, d), lambda i, e: (0, 0)),
                  pl.BlockSpec((tm, LANES), lambda i, e: (i, 0)),
                  pl.BlockSpec((1, d, ffe), lambda i, e: (e, 0, 0)),
                  pl.BlockSpec((1, d, ffe), lambda i, e: (e, 0, 0)),
                  pl.BlockSpec((1, ffe, d), lambda i, e: (e, 0, 0))],
        out_specs=pl.BlockSpec((tm, d), lambda i, e: (i, 0)),
        out_shape=jax.ShapeDtypeStruct((n, d), F32),
        scratch_shapes=[pltpu.VMEM((tm, d), F32 if precise else BF16), pltpu.VMEM((tm, d), F32)],
        compiler_params=_cparams("parallel", "arbitrary"),
        name="moe_ffn",
    )(h, g_row, gates, wg, wu, wd)


def _head_rows(width):
    r = lax.broadcasted_iota(jnp.int32, (8, width), 0)
    c = lax.broadcasted_iota(jnp.int32, (8, width), 1)
    return (c // HEAD_DIM) == r


def _lane_to_rows(row_vec):
    r = lax.broadcasted_iota(jnp.int32, (8, LANES), 0)
    c = lax.broadcasted_iota(jnp.int32, (8, LANES), 1)
    return jnp.sum(jnp.where(r == c, row_vec, 0.0), axis=1, keepdims=True)


def _finish_decode(acc, own, g_ref, o_ref):
    o = jnp.sum(jnp.where(own, acc, 0.0), axis=0, keepdims=True)
    o_ref[0] = _rms(o, g_ref[...])


def _sb_decode_kernel(pt_ref, q_ref, g_ref, *rest, pps):
    k_refs, v_refs = rest[:pps], rest[pps:2 * pps]
    o_ref, acc_ref, r_ref = rest[2 * pps:]
    c = pl.program_id(1)

    @pl.when(c == 0)
    def _():
        acc_ref[...] = jnp.zeros_like(acc_ref)
        r_ref[...] = jnp.zeros_like(r_ref)

    own = _head_rows(ATT_W)
    qb = jnp.where(own, q_ref[0] * (HEAD_DIM ** -0.5), 0.0)
    u = _tri(LANES, lambda r, cc: r >= cc)
    order = list(reversed(range(pps)))
    zs = [_dot3(qb, k_refs[i][0]) for i in order]
    css = [_dot_ones(_softplus(z), u) for z in zs]
    acc = acc_ref[...]
    run = r_ref[...]
    for i, z, cs in zip(order, zs, css):
        acc = acc + _dot3_nt(jnp.exp(z - cs - run), v_refs[i][0])
        run = run + cs[:, 0:1]
    acc_ref[...] = acc
    r_ref[...] = run

    @pl.when(c == pl.num_programs(1) - 1)
    def _():
        _finish_decode(acc, own, g_ref, o_ref)


def _page_specs(n_chunks, pps, page_off, block):
    def spec(i):
        def index(b, c, pt):
            return (page_off + pt[b, (n_chunks - 1 - c) * pps + i],) + (0,) * (len(block) - 1)
        return pl.BlockSpec(block, index)
    return [spec(i) for i in range(pps)]


def sb_decode(q3, g_row, cache_k, cache_v, page_table, *, page_off, pps):
    nb = q3.shape[0]
    n_pages = page_table.shape[1]
    n_chunks = n_pages // pps
    page = cache_k.shape[2]
    kern = functools.partial(_sb_decode_kernel, pps=pps)
    pages = _page_specs(n_chunks, pps, page_off, (1, ATT_W, page))
    grid_spec = pltpu.PrefetchScalarGridSpec(
        num_scalar_prefetch=1,
        grid=(nb, n_chunks),
        in_specs=[pl.BlockSpec((1, 1, ATT_W), lambda b, c, pt: (b, 0, 0)),
                  pl.BlockSpec((1, ATT_W), lambda b, c, pt: (0, 0))] + pages + pages,
        out_specs=pl.BlockSpec((1, 1, ATT_W), lambda b, c, pt: (b, 0, 0)),
        scratch_shapes=[pltpu.VMEM((8, ATT_W), F32), pltpu.VMEM((8, 1), F32)],
    )
    return pl.pallas_call(
        kern,
        grid_spec=grid_spec,
        out_shape=jax.ShapeDtypeStruct((nb, 1, ATT_W), F32),
        compiler_params=_cparams("parallel", "arbitrary"),
        name="sb_decode",
    )(page_table, q3, g_row, *([cache_k] * pps), *([cache_v] * pps))


def _fox_decode_kernel(pt_ref, q_ref, k_ref, v_ref, f_ref, gq_ref, gk_ref, fb_ref, g_ref, *rest, pps):
    k_refs, v_refs, f_refs = rest[:pps], rest[pps:2 * pps], rest[2 * pps:3 * pps]
    o_ref, kn_ref, lf_ref, acc_ref, m_ref, l_ref, r_ref, qb_ref = rest[3 * pps:]
    c = pl.program_id(1)
    own = _head_rows(ATT_W)

    @pl.when(c == 0)
    def _():
        def head_norm(x_row, gain_row):
            xb = jnp.where(own, x_row, 0.0)
            ms = jnp.sum(xb * xb, axis=1, keepdims=True) * (1.0 / HEAD_DIM)
            return xb * lax.rsqrt(ms + EPS) * gain_row

        qn = head_norm(q_ref[0], gq_ref[...]) * (HEAD_DIM ** -0.5)
        kn = head_norm(k_ref[0], gk_ref[...])
        kn_ref[0] = jnp.sum(kn, axis=0, keepdims=True)
        fx = f_ref[0] + fb_ref[...]
        lf = jnp.minimum(fx, 0.0) - jnp.log(1.0 + jnp.exp(-jnp.abs(fx)))
        lf_ref[0] = lf
        qb_ref[...] = qn
        m_ref[...] = jnp.sum(qn * kn, axis=1, keepdims=True)
        l_ref[...] = jnp.ones_like(l_ref)
        acc_ref[...] = jnp.broadcast_to(v_ref[0], acc_ref.shape)
        r_ref[...] = _lane_to_rows(lf)

    qb = qb_ref[...]
    u = _tri(LANES, lambda r, cc: r >= cc)
    m_old, run = m_ref[...], r_ref[...]
    pad = jnp.zeros((8 - N_HEADS, LANES), F32)
    order = list(reversed(range(pps)))
    zs = [_dot3(qb, k_refs[i][0]) for i in order]
    lfs =[jnp.concatenate([f_refs[i][0], pad], axis=0) for i in order]
    css = [_dot_ones(lf, u) for lf in lfs]
    ss = []
    for z, lf, cs in zip(zs, lfs, css):
        ss.append(z + (cs - lf) + run)
        run = run + cs[:, 0:1]
    m_new = m_old
    for s in ss:
        m_new = jnp.maximum(m_new, jnp.max(s, axis=1, keepdims=True))
    alpha = jnp.exp(m_old - m_new)
    l = alpha * l_ref[...]
    acc = alpha * acc_ref[...]
    for i, s in zip(order, ss):
        pr = jnp.exp(s - m_new)
        l = l + jnp.sum(pr, axis=1, keepdims=True)
        acc = acc + _dot3_nt(pr, v_refs[i][0])
    acc_ref[...], m_ref[...], l_ref[...], r_ref[...] = acc, m_new, l, run

    @pl.when(c == pl.num_programs(1) - 1)
    def _():
        _finish_decode(acc / l, own, g_ref, o_ref)


def fox_decode(q3, k3, v3, f3, gq_row, gk_row, fb_row, g_row, cache_k, cache_v, cache_ft, page_table,
               *, page_off, pps):
    nb = q3.shape[0]
    n_pages = page_table.shape[1]
    n_chunks = n_pages // pps
    page = cache_k.shape[2]
    kern = functools.partial(_fox_decode_kernel, pps=pps)
    pages = _page_specs(n_chunks, pps, page_off, (1, ATT_W, page))
    f_pages = _page_specs(n_chunks, pps, page_off, (1, N_HEADS, page))
    tok = lambda w: pl.BlockSpec((1, 1, w), lambda b, c, pt: (b, 0, 0))
    par = lambda w: pl.BlockSpec((1, w), lambda b, c, pt: (0, 0))
    grid_spec = pltpu.PrefetchScalarGridSpec(
        num_scalar_prefetch=1,
        grid=(nb, n_chunks),
        in_specs=[tok(ATT_W), tok(ATT_W), tok(ATT_W), tok(LANES),
                  par(ATT_W), par(ATT_W), par(LANES), par(ATT_W)] + pages + pages + f_pages,
        out_specs=[tok(ATT_W), tok(ATT_W), tok(LANES)],
        scratch_shapes=[pltpu.VMEM((8, ATT_W), F32), pltpu.VMEM((8, 1), F32), pltpu.VMEM((8, 1), F32),
                        pltpu.VMEM((8, 1), F32), pltpu.VMEM((8, ATT_W), F32)],
    )
    return pl.pallas_call(
        kern,
        grid_spec=grid_spec,
        out_shape=[jax.ShapeDtypeStruct((nb, 1, ATT_W), F32),
                   jax.ShapeDtypeStruct((nb, 1, ATT_W), F32),
                   jax.ShapeDtypeStruct((nb, 1, LANES), F32)],
        compiler_params=_cparams("parallel", "arbitrary"),
        name="fox_decode",
    )(page_table, q3, k3, v3, f3, gq_row, gk_row, fb_row, g_row,
      *([cache_k] * pps), *([cache_v] * pps), *([cache_ft] * pps))


def _row_to_col(row_vec):
    r = lax.broadcasted_iota(jnp.int32, (HG_D, HG_D), 0)
    c = lax.broadcasted_iota(jnp.int32, (HG_D, HG_D), 1)
    return jnp.sum(jnp.where(r == c, row_vec, 0.0), axis=1, keepdims=True)


def _hgrn_decode_kernel(hq_ref, hf_ref, hi_ref, hg_ref, lb_ref, g_ref, s_ref, o_ref, sn_ref):
    outs = []
    for h in range(N_HEADS):
        ws = slice(h * HG_D, (h + 1) * HG_D)
        q, f, k = _hgrn_gates(hq_ref[0][:, ws], hf_ref[0][:, ws], lb_ref[:, ws])
        s_new = s_ref[0, h] * _row_to_col(f) + _row_to_col(k) * hi_ref[0][:, ws]
        sn_ref[0, h] = s_new
        o = jnp.sum(_row_to_col(q) * s_new, axis=0, keepdims=True)
        outs.append(_rms(o, g_ref[...]) * _sigmoid(hg_ref[0][:, ws]))
    o_ref[0] = jnp.concatenate(outs, axis=1)


def hgrn_decode(hq3, hf3, hi3, hg3, lb_row, g_row, state):
    nb = hq3.shape[0]
    tok = pl.BlockSpec((1, 1, HG_W), lambda b: (b, 0, 0))
    st = pl.BlockSpec((1, N_HEADS, HG_D, HG_D), lambda b: (b, 0, 0, 0))
    return pl.pallas_call(
        _hgrn_decode_kernel,
        grid=(nb,),
        in_specs=[tok, tok, tok, tok,
                  pl.BlockSpec((1, HG_W), lambda b: (0, 0)),
                  pl.BlockSpec((1, HG_D), lambda b: (0, 0)), st],
        out_specs=[tok, st],
        out_shape=[jax.ShapeDtypeStruct((nb, 1, HG_W), F32),
                   jax.ShapeDtypeStruct(state.shape, F32)],
        compiler_params=_cparams("parallel"),
        name="hgrn_decode",
    )(hq3, hf3, hi3, hg3, lb_row, g_row, state)


def _arrange_w_in(w_in_l):
    d = w_in_l.shape[0]
    a = 2 * 3 * ATT_W
    main = jnp.concatenate([w_in_l[:, :a], w_in_l[:, a + N_HEADS:]], axis=1)
    fcols = w_in_l[:, a:a + N_HEADS]
    pad = jnp.zeros((d, N_IN_PAD - C_FXF - N_HEADS), w_in_l.dtype)
    return jnp.concatenate([main, fcols, pad], axis=1).astype(F32)


def _seg_matrix():
    r = lax.broadcasted_iota(jnp.int32, (ATT_W, ATT_W), 0) // HEAD_DIM
    c = lax.broadcasted_iota(jnp.int32, (ATT_W, ATT_W), 1) // HEAD_DIM
    return jnp.where(r == c, 1.0, 0.0).astype(BF16)


def _row(v, width=None):
    v = v.reshape(1, -1).astype(F32)
    if width is not None and v.shape[1] < width:
        v = jnp.pad(v, ((0, 0), (0, width - v.shape[1])))
    return v


def kernel(x_prompt, x_sample, cache_sb_k, cache_sb_v, cache_fox_k, cache_fox_v, cache_fox_logf, state_hgrn,
           page_table, w_in, w_out, norm_mix_g, norm_ffn_g, fox_q_norm_g, fox_k_norm_g, fox_f_bias, sb_out_g,
           fox_out_g, hgrn_out_g, hgrn_lb_logits, ffn_w_gate, ffn_w_up, ffn_w_down, moe_router_w,
           moe_router_b, moe_w_gate, moe_w_up, moe_w_down):
    batch, seq, d_model = x_prompt.shape
    dec_b = x_sample.shape[0]
    depth = w_in.shape[0]
    n_pool, page = cache_sb_k.shape[1], cache_sb_k.shape[2]
    n_experts = moe_router_w.shape[-1]

    lb_all = lower_bounds(hgrn_lb_logits.astype(F32))
    seg = _seg_matrix()
    def pages_t(c):
        return jnp.transpose(c, (0, 1, 3, 4, 2)).reshape(depth * n_pool, ATT_W, page)

    sbk_c, sbv_c, fxk_c, fxv_c = (pages_t(c) for c in (cache_sb_k, cache_sb_v, cache_fox_k, cache_fox_v))
    fxf_c = jnp.swapaxes(cache_fox_logf, 2, 3).reshape(depth * n_pool, N_HEADS, page)

    xp = x_prompt.reshape(batch * seq, d_model)
    xs = x_sample.reshape(dec_b, d_model)
    p_out = [[] for _ in range(6)]
    s_out = [[] for _ in range(6)]
    kv_stacks, kn_stack = None, None

    for l in range(depth):
        w_in_f = _arrange_w_in(w_in[l])
        w_in_l = w_in_f.astype(BF16)
        w_out_f = w_out[l].astype(F32)
        w_out_l = w_out_f.astype(BF16)
        g_mix, g_ffn = _row(norm_mix_g[l]), _row(norm_ffn_g[l])
        gq = _row(jnp.tile(fox_q_norm_g[l], N_HEADS))
        gk = _row(jnp.tile(fox_k_norm_g[l], N_HEADS))
        fb = _row(fox_f_bias[l], LANES)
        g_sb, g_fx, g_hg = _row(sb_out_g[l]), _row(fox_out_g[l]), _row(hgrn_out_g[l])
        lb_row = lb_all[l:l + 1]

        p, kv_stacks = in_proj_stacked(xp, g_mix, w_in_l, kv_stacks, l, depth, tm=512, tn=1280)
        kn_stack, lf, qa, ka = fox_prep(p, gq, gk, fb, seg, kn_stack, l, depth,
                                        batch=batch, seq=seq, tb=256)
        sb_o = sb_attention(p, g_sb, batch=batch, seq=seq, tq=512, tk=256)
        fx_o = fox_attention(qa, ka, p, g_fx, batch=batch, seq=seq, tq=256)
        hg_o, s_fin = hgrn_prompt(p, lb_row, g_hg, batch=batch, seq=seq, chunk=128, sub=8)
        hp = out_proj(xp, sb_o, fx_o, hg_o, w_out_l, tm=512)
        p_out[4].append(lf)
        p_out[5].append(s_fin)

        ps = in_proj(xs, g_mix, w_in_f, tm=dec_b, tn=1280)
        tok = lambda c0, w: ps[:, c0:c0 + w].reshape(dec_b, 1, w)
        off = l * n_pool
        sb_os = sb_decode(tok(C_SBQ, ATT_W), g_sb, sbk_c, sbv_c, page_table, page_off=off, pps=32)
        fx_os, fxk_s, lf_s = fox_decode(tok(C_FXQ, ATT_W), tok(C_FXK, ATT_W), tok(C_FXV, ATT_W),
                                        tok(C_FXF, LANES), gq, gk, fb, g_fx, fxk_c, fxv_c, fxf_c,
                                        page_table, page_off=off, pps=32)
        hg_os, s_new = hgrn_decode(tok(C_HGQ, HG_W), tok(C_HGF, HG_W), tok(C_HGI, HG_W), tok(C_HGG, HG_W),
                                   lb_row, g_hg, state_hgrn[l].astype(F32))
        hs = out_proj(xs, sb_os.reshape(dec_b, ATT_W), fx_os.reshape(dec_b, ATT_W),
                      hg_os.reshape(dec_b, HG_W), w_out_f, tm=dec_b)
        s_out[0].append(ps[:, C_SBK:C_SBK + ATT_W])
        s_out[1].append(ps[:, C_SBV:C_SBV + ATT_W])
        s_out[2].append(fxk_s.reshape(dec_b, ATT_W))
        s_out[3].append(ps[:, C_FXV:C_FXV + ATT_W])
        s_out[4].append(lf_s.reshape(dec_b, LANES)[:, :N_HEADS])
        s_out[5].append(s_new)

        if l % 2 == 0:
            wf = [w[l // 2].astype(F32) for w in (ffn_w_gate, ffn_w_up, ffn_w_down)]
            wg, wu, wd = (w.astype(BF16) for w in wf)
            xp = ffn_dense(hp, g_ffn, wg, wu, wd, tm=1024, tf=256)
            xs = ffn_dense(hs, g_ffn, *wf, tm=dec_b, tf=256)
        else:
            wf = [w[l // 2].astype(F32) for w in (moe_w_gate, moe_w_up, moe_w_down)]
            wg, wu, wd = (w.astype(BF16) for w in wf)
            rw = jnp.pad(moe_router_w[l // 2].astype(F32), ((0, 0), (0, LANES - n_experts)))
            rb = _row(moe_router_b[l // 2], LANES)
            gates_p = moe_router(hp, g_ffn, rw, rb, tm=512, n_experts=n_experts)
            xp = moe_ffn(hp, g_ffn, gates_p, wg, wu, wd, tm=512)
            gates_s = moe_router(hs, g_ffn, rw, rb, tm=dec_b, n_experts=n_experts)
            xs = moe_ffn(hs, g_ffn, gates_s, *wf, tm=dec_b)

    def stack(parts, shape):
        return jnp.stack(parts).reshape((depth,) + shape)

    hd = (N_HEADS, HEAD_DIM)
    y_prompt = xp.reshape(batch, seq, d_model)
    y_sample = xs.reshape(dec_b, 1, d_model)
    sbk_p, sbv_p, fxv_p = kv_stacks
    outs_p = [a.reshape((depth, batch, seq) + hd) for a in (sbk_p, sbv_p, kn_stack, fxv_p)]
    outs_p.append(stack(p_out[4], (batch, seq, N_HEADS)))
    outs_p.append(stack(p_out[5], (batch, N_HEADS, HG_D, HG_D)).astype(state_hgrn.dtype))
    outs_s = [stack(s_out[i], (dec_b, 1) + hd) for i in range(4)]
    outs_s.append(stack(s_out[4], (dec_b, 1, N_HEADS)))
    outs_s.append(stack(s_out[5], (dec_b, N_HEADS, HG_D, HG_D)).astype(state_hgrn.dtype))
    return (y_prompt, y_sample, *outs_p, *outs_s)
```

```python
import functools

import jax
import jax.numpy as jnp
from jax import lax
from jax.experimental import pallas as pl
from jax.experimental.pallas import tpu as pltpu

F32 = jnp.float32
BF16 = jnp.bfloat16
EPS = 1e-6
NEG_BIG = -1e30

HEAD_DIM = 64
N_HEADS = 4
ATT_W = N_HEADS * HEAD_DIM
HG_D = 128
HG_W = N_HEADS * HG_D
LANES = 128
VMEM_LIMIT = 56 * 1024 * 1024

C_SBQ, C_SBK, C_SBV = 0, 256, 512
C_FXQ, C_FXK, C_FXV = 768, 1024, 1280
C_HGQ, C_HGF, C_HGI, C_HGG = 1536, 2048, 2560, 3072
C_FXF = 3584
N_IN_PAD = 3840


def _cparams(*sem):
    return pltpu.CompilerParams(dimension_semantics=sem, vmem_limit_bytes=VMEM_LIMIT)


def _rms(xf, g_row):
    ms = jnp.mean(xf * xf, axis=-1, keepdims=True)
    return xf * lax.rsqrt(ms + EPS) * g_row


def _softplus(z):
    return jnp.maximum(z, 0.0) + jnp.log(1.0 + jnp.exp(-jnp.abs(z)))


def _sigmoid(x):
    return 1.0 / (1.0 + jnp.exp(-x))


def _split2(x):
    hi = x.astype(BF16)
    lo = (x - hi.astype(F32)).astype(BF16)
    return hi, lo


def _split3(x):
    hi = x.astype(BF16)
    r = x - hi.astype(F32)
    mid = r.astype(BF16)
    lo = (r - mid.astype(F32)).astype(BF16)
    return hi, mid, lo


def _dot(a, b):
    return jnp.dot(a, b, preferred_element_type=F32)


def _dot_nt(a, b):
    return lax.dot_general(a, b, (((1,), (1,)), ((), ())), preferred_element_type=F32)


def _stack_split2(x):
    hi, lo = _split2(x)
    return hi, jnp.concatenate([hi, lo], axis=0)


def _dot3(x, w):
    m = x.shape[0]
    xh, x2 = _stack_split2(x)
    wh, wl = _split2(w)
    r = _dot(x2, wh)
    return r[:m] + r[m:] + _dot(xh, wl)


def _dot3_nt(x, w):
    m = x.shape[0]
    xh, x2 = _stack_split2(x)
    wh, wl = _split2(w)
    r = _dot_nt(x2, wh)
    return r[:m] + r[m:] + _dot_nt(xh, wl)


def _dot_ones(x, ones_bf16):
    m = x.shape[0]
    r = _dot(jnp.concatenate(_split3(x), axis=0), ones_bf16)
    return r[:m] + r[m:2 * m] + r[2 * m:]


def _mm(x, w, precise):
    return _dot3(x, w) if precise else _dot(x.astype(BF16), w)


def _tri(n, rel):
    r = lax.broadcasted_iota(jnp.int32, (n, n), 0)
    c = lax.broadcasted_iota(jnp.int32, (n, n), 1)
    return jnp.where(rel(r, c), 1.0, 0.0).astype(BF16)


def _lower_bounds_kernel(x_ref, o_ref):
    x = x_ref[...]
    depth = x.shape[0]
    m = jnp.max(x, axis=0, keepdims=True)
    e = jnp.exp(x - m)
    sm = e / jnp.sum(e, axis=0, keepdims=True)
    run = jnp.zeros_like(sm[0:1])
    rows = []
    for l in range(depth):
        run = run + sm[l:l + 1]
        rows.append(run - sm[0:1])
    o_ref[...] = jnp.concatenate(rows, axis=0)


def lower_bounds(logits):
    return pl.pallas_call(
        _lower_bounds_kernel,
        out_shape=jax.ShapeDtypeStruct(logits.shape, F32),
        name="hgrn_lower_bounds",
    )(logits)


def _in_proj_kernel(x_ref, g_ref, w_ref, o_ref, xn_ref, *, precise):
    @pl.when(pl.program_id(1) == 0)
    def _():
        xn_ref[...] = _rms(x_ref[...], g_ref[...]).astype(xn_ref.dtype)

    o_ref[...] = _mm(xn_ref[...], w_ref[...], precise)


def in_proj(x, g_row, w, *, tm, tn):
    n, d = x.shape
    n_out = w.shape[1]
    precise = w.dtype == F32
    return pl.pallas_call(
        functools.partial(_in_proj_kernel, precise=precise),
        grid=(n // tm, n_out // tn),
        in_specs=[pl.BlockSpec((tm, d), lambda i, j: (i, 0)),
                  pl.BlockSpec((1, d), lambda i, j: (0, 0)),
                  pl.BlockSpec((d, tn), lambda i, j: (0, j))],
        out_specs=pl.BlockSpec((tm, tn), lambda i, j: (i, j)),
        out_shape=jax.ShapeDtypeStruct((n, n_out), F32),
        scratch_shapes=[pltpu.VMEM((tm, d), F32 if precise else BF16)],
        compiler_params=_cparams("parallel", "arbitrary"),
        name="norm_in_proj",
    )(x, g_row, w)


STACKED_COLS = (C_SBK, C_SBV, C_FXV)


def _in_proj_stack_kernel(x_ref, g_ref, w_ref, *rest, tn, n_alias):
    o_ref, s_refs, xn_ref = rest[n_alias], rest[n_alias + 1:-1], rest[-1]
    j = pl.program_id(1)

    @pl.when(j == 0)
    def _():
        xn_ref[...] = _rms(x_ref[...], g_ref[...]).astype(BF16)

    res = _dot(xn_ref[...], w_ref[...])
    o_ref[...] = res
    for c0, s_ref in zip(STACKED_COLS, s_refs):
        @pl.when(j == c0 // tn)
        def _(c0=c0, s_ref=s_ref):
            s_ref[0, 0] = res[:, c0 % tn:c0 % tn + ATT_W].T


def in_proj_stacked(x, g_row, w_bf16, stacks, layer, depth, batch, *, tm, tn):
    n, d = x.shape
    n_out = w_bf16.shape[1]
    seq = n // batch
    tiles = seq // tm
    assert all(c0 % tn + ATT_W <= tn for c0 in STACKED_COLS) and seq % tm == 0
    n_alias = 0 if stacks is None else len(STACKED_COLS)
    stack_sds = jax.ShapeDtypeStruct((depth, batch, ATT_W, seq), F32)
    outs = pl.pallas_call(
        functools.partial(_in_proj_stack_kernel, tn=tn, n_alias=n_alias),
        grid=(n // tm, n_out // tn),
        in_specs=[pl.BlockSpec((tm, d), lambda i, j: (i, 0)),
                  pl.BlockSpec((1, d), lambda i, j: (0, 0)),
                  pl.BlockSpec((d, tn), lambda i, j: (0, j))]
                 + [pl.BlockSpec(memory_space=pl.ANY)] * n_alias,
        out_specs=[pl.BlockSpec((tm, tn), lambda i, j: (i, j))]
                  + [pl.BlockSpec((1, 1, ATT_W, tm), lambda i, j: (layer, i // tiles, 0, i % tiles))]
                  * len(STACKED_COLS),
        out_shape=[jax.ShapeDtypeStruct((n, n_out), F32)] + [stack_sds] * len(STACKED_COLS),
        input_output_aliases={3 + a: 1 + a for a in range(n_alias)},
        scratch_shapes=[pltpu.VMEM((tm, d), BF16)],
        compiler_params=_cparams("parallel", "arbitrary"),
        name="norm_in_proj_stacked",
    )(x, g_row, w_bf16, *(stacks or ()))
    return outs[0], tuple(outs[1:])


def _seg_mean_sq(x, seg_ref):
    hi, lo = _split2(x * x)
    return (_dot(hi, seg_ref[...]) + _dot(lo, seg_ref[...])) * (1.0 / HEAD_DIM)


def _fox_prep_kernel(q_ref, k_ref, f_ref, gq_ref, gk_ref, fb_ref, seg_ref, *rest):
    kn_ref, lf_ref, qa_ref, ka_ref, carry_ref = rest[-5:]
    tb = q_ref.shape[0]

    @pl.when(pl.program_id(1) == 0)
    def _():
        carry_ref[...] = jnp.zeros_like(carry_ref)

    xq = q_ref[...]
    xk = k_ref[...]
    qn = xq * lax.rsqrt(_seg_mean_sq(xq, seg_ref) + EPS) * gq_ref[...]
    kn = xk * lax.rsqrt(_seg_mean_sq(xk, seg_ref) + EPS) * gk_ref[...]
    kn_ref[0, 0] = kn.T

    fx = f_ref[...] + fb_ref[...]
    lf = jnp.minimum(fx, 0.0) - jnp.log(1.0 + jnp.exp(-jnp.abs(fx)))
    lf_ref[0, 0] = lf.T[:N_HEADS]

    ltri = _tri(tb, lambda r, c: c <= r)
    p0, p1, p2 = _split3(lf)
    cum = _dot(ltri, p0) + _dot(ltri, p1) + _dot(ltri, p2) + carry_ref[...]
    carry_ref[...] = cum[tb - 1:tb, :]

    c0, c1, c2 = (c.astype(F32) for c in _split3(cum))
    lane = lax.broadcasted_iota(jnp.int32, (tb, HEAD_DIM), 1)
    one = jnp.where(lane < 6, 1.0, 0.0)
    qs = qn * (HEAD_DIM ** -0.5)
    q_parts, k_parts = [], []
    for h in range(N_HEADS):
        a0, a1, a2 = c0[:, h:h + 1], c1[:, h:h + 1], c2[:, h:h + 1]
        ext_q = jnp.where(lane == 0, a0, jnp.where(lane == 1, a1, jnp.where(lane == 2, a2, one)))
        ext_k = jnp.where(lane == 3, -a0, jnp.where(lane == 4, -a1, jnp.where(lane == 5, -a2, one)))
        q_parts += [qs[:, h * HEAD_DIM:(h + 1) * HEAD_DIM], ext_q]
        k_parts += [kn[:, h * HEAD_DIM:(h + 1) * HEAD_DIM], ext_k]
    qa_ref[...] = jnp.concatenate(q_parts, axis=1).astype(BF16)
    ka_ref[...] = jnp.concatenate(k_parts, axis=1).astype(BF16)


def fox_prep(p, gq_row, gk_row, fb_row, seg, stacks, layer, depth, *, batch, seq, tb):
    n = p.shape[0]
    nb = seq // tb
    row = lambda b, t: b * nb + t
    aw = N_HEADS * LANES
    alias = () if stacks is None else tuple(stacks)
    return pl.pallas_call(
        _fox_prep_kernel,
        grid=(batch, nb),
        in_specs=[pl.BlockSpec((tb, ATT_W), lambda b, t: (row(b, t), C_FXQ // ATT_W)),
                  pl.BlockSpec((tb, ATT_W), lambda b, t: (row(b, t), C_FXK // ATT_W)),
                  pl.BlockSpec((tb, LANES), lambda b, t: (row(b, t), C_FXF // LANES)),
                  pl.BlockSpec((1, ATT_W), lambda b, t: (0, 0)),
                  pl.BlockSpec((1, ATT_W), lambda b, t: (0, 0)),
                  pl.BlockSpec((1, LANES), lambda b, t: (0, 0)),
                  pl.BlockSpec((ATT_W, ATT_W), lambda b, t: (0, 0))]
                 + [pl.BlockSpec(memory_space=pl.ANY)] * len(alias),
        out_specs=[pl.BlockSpec((1, 1, ATT_W, tb), lambda b, t: (layer, b, 0, t)),
                   pl.BlockSpec((1, 1, N_HEADS, tb), lambda b, t: (layer, b, 0, t)),
                   pl.BlockSpec((tb, aw), lambda b, t: (row(b, t), 0)),
                   pl.BlockSpec((tb, aw), lambda b, t: (row(b, t), 0))],
        out_shape=[jax.ShapeDtypeStruct((depth, batch, ATT_W, seq), F32),
                   jax.ShapeDtypeStruct((depth, batch, N_HEADS, seq), F32),
                   jax.ShapeDtypeStruct((n, aw), BF16),
                   jax.ShapeDtypeStruct((n, aw), BF16)],
        input_output_aliases={7: 0, 8: 1} if alias else {},
        scratch_shapes=[pltpu.VMEM((1, LANES), F32)],
        compiler_params=_cparams("parallel", "arbitrary"),
        name="fox_prep",
    )(p, p, p, gq_row, gk_row, fb_row, seg, *alias)


def _store_v_transposed(v_ref, vt_ref, tk):
    for j in range(vt_ref.shape[0]):
        vt_ref[j] = v_ref[j * tk:(j + 1) * tk, :].T.astype(BF16)


def _finish_attention(ot, g_ref, o_ref):
    ms = jnp.mean(ot * ot, axis=0, keepdims=True)
    o_ref[...] = (ot * lax.rsqrt(ms + EPS)).T * g_ref[...]


def _sb_attn_kernel(q_ref, k_ref, v_ref, g_ref, o_ref, acc_ref, r_ref, *, tk):
    i = pl.program_id(1)
    tq = q_ref.shape[0]
    acc_ref[...] = jnp.zeros_like(acc_ref)
    r_ref[...] = jnp.zeros_like(r_ref)
    u = _tri(tk, lambda r, c: r >= c)
    n_diag = tq // tk
    qry_i = lax.broadcasted_iota(jnp.int32, (tq, tk), 0)
    key_i = lax.broadcasted_iota(jnp.int32, (tq, tk), 1)
    low_half = lax.broadcasted_iota(jnp.int32, (tq, LANES), 1) < HEAD_DIM
    qm = []
    for pair in range(N_HEADS // 2):
        qp = q_ref[:, pair * LANES:(pair + 1) * LANES] * (HEAD_DIM ** -0.5)
        qm.append(jnp.where(low_half, qp, 0.0).astype(BF16))
        qm.append(jnp.where(low_half, 0.0, qp).astype(BF16))

    def block(j, diag_off):
        row0 = pl.multiple_of(j * tk, tk)
        before = None if diag_off is None else (key_i + diag_off < qry_i)
        for pair in range(N_HEADS // 2):
            ls = slice(pair * LANES, (pair + 1) * LANES)
            kp = k_ref[pl.ds(row0, tk), ls].astype(BF16)
            vp = v_ref[pl.ds(row0, tk), ls].astype(BF16)
            res = []
            for h in (2 * pair, 2 * pair + 1):
                z = _dot_nt(qm[h], kp)
                sp = _softplus(z)
                if before is not None:
                    sp = jnp.where(before, sp, 0.0)
                cs = _dot(sp.astype(BF16), u)
                r_old = r_ref[h]
                a = jnp.exp(z - cs - r_old)
                if before is not None:
                    a = jnp.where(before, a, 0.0)
                res.append(_dot(a.astype(BF16), vp))
                r_ref[h] = r_old + cs[:, 0:1]
            acc_ref[:, ls] += jnp.where(low_half, res[0], res[1])

    for d in reversed(range(n_diag)):
        block(i * n_diag + d, d * tk)

    def body(jj, c):
        block(i * n_diag - 1 - jj, None)
        return c

    lax.fori_loop(0, i * n_diag, body, 0)
    o_ref[...] = _rms(acc_ref[...], g_ref[...])


def sb_attention(p, g_row, *, batch, seq, tq, tk):
    n = p.shape[0]
    nq = seq // tq
    assert tq % tk == 0
    kern = functools.partial(_sb_attn_kernel, tk=tk)
    return pl.pallas_call(
        kern,
        grid=(batch, nq),
        in_specs=[pl.BlockSpec((tq, ATT_W), lambda b, i: (b * nq + i, C_SBQ // ATT_W)),
                  pl.BlockSpec((seq, ATT_W), lambda b, i: (b, C_SBK // ATT_W)),
                  pl.BlockSpec((seq, ATT_W), lambda b, i: (b, C_SBV // ATT_W)),
                  pl.BlockSpec((1, ATT_W), lambda b, i: (0, 0))],
        out_specs=pl.BlockSpec((tq, ATT_W), lambda b, i: (b * nq + i, 0)),
        out_shape=jax.ShapeDtypeStruct((n, ATT_W), F32),
        scratch_shapes=[pltpu.VMEM((tq, ATT_W), F32),
                        pltpu.VMEM((N_HEADS, tq, 1), F32)],
        compiler_params=_cparams("parallel", "arbitrary"),
        name="sb_attention",
    )(p, p, p, g_row)


def _fox_attn_kernel(q_ref, k_ref, v_ref, g_ref, o_ref, vt_ref, acc_ref, m_ref, l_ref, *, tk):
    i = pl.program_id(1)
    tq = q_ref.shape[0]

    @pl.when(i == 0)
    def _():
        _store_v_transposed(v_ref, vt_ref, tk)

    acc_ref[...] = jnp.zeros_like(acc_ref)
    m_ref[...] = jnp.full_like(m_ref, NEG_BIG)
    l_ref[...] = jnp.zeros_like(l_ref)
    key_i = lax.broadcasted_iota(jnp.int32, (tk, tq), 0)
    qry_i = lax.broadcasted_iota(jnp.int32, (tk, tq), 1)
    causal = key_i <= qry_i

    def block(j, diagonal):
        row0 = pl.multiple_of(j * tk, tk)
        sts = [_dot_nt(k_ref[pl.ds(row0, tk), h * LANES:(h + 1) * LANES],
                       q_ref[:, h * LANES:(h + 1) * LANES]) for h in range(N_HEADS)]
        for h in range(N_HEADS):
            hs = slice(h * HEAD_DIM, (h + 1) * HEAD_DIM)
            st = sts[h]
            if diagonal:
                st = jnp.where(causal, st, NEG_BIG)
            m_old = m_ref[h:h + 1, :]
            m_new = jnp.maximum(m_old, jnp.max(st, axis=0, keepdims=True))
            alpha = jnp.exp(m_old - m_new)
            pt = jnp.exp(st - m_new)
            l_ref[h:h + 1, :] = alpha * l_ref[h:h + 1, :] + jnp.sum(pt, axis=0, keepdims=True)
            acc_ref[hs, :] = alpha * acc_ref[hs, :] + _dot(vt_ref[j, hs, :], pt.astype(BF16))
            m_ref[h:h + 1, :] = m_new

    block(i, True)

    def body(jj, c):
        block(i - 1 - jj, False)
        return c

    lax.fori_loop(0, i, body, 0)
    parts = [acc_ref[h * HEAD_DIM:(h + 1) * HEAD_DIM, :] / l_ref[h:h + 1, :] for h in range(N_HEADS)]
    _finish_attention(jnp.concatenate(parts, axis=0), g_ref, o_ref)


def fox_attention(qa, ka, p, g_row, *, batch, seq, tq):
    n = p.shape[0]
    nq = seq // tq
    aw = N_HEADS * LANES
    kern = functools.partial(_fox_attn_kernel, tk=tq)
    return pl.pallas_call(
        kern,
        grid=(batch, nq),
        in_specs=[pl.BlockSpec((tq, aw), lambda b, i: (b * nq + i, 0)),
                  pl.BlockSpec((seq, aw), lambda b, i: (b, 0)),
                  pl.BlockSpec((seq, ATT_W), lambda b, i: (b, C_FXV // ATT_W)),
                  pl.BlockSpec((1, ATT_W), lambda b, i: (0, 0))],
        out_specs=pl.BlockSpec((tq, ATT_W), lambda b, i: (b * nq + i, 0)),
        out_shape=jax.ShapeDtypeStruct((n, ATT_W), F32),
        scratch_shapes=[pltpu.VMEM((nq, ATT_W, tq), BF16),
                        pltpu.VMEM((ATT_W, tq), F32),
                        pltpu.VMEM((8, tq), F32),
                        pltpu.VMEM((8, tq), F32)],
        compiler_params=_cparams("parallel", "arbitrary"),
        name="fox_attention",
    )(qa, ka, p, g_row)


def _hgrn_gates(hq, hf, lb):
    q = hq * _sigmoid(hq)
    f = lb + (1.0 - lb) * _sigmoid(hf)
    return q, f, 1.0 - f


def _hgrn_kernel(hq_ref, hf_ref, hi_ref, hg_ref, lb_ref, g_ref, o_ref, s_ref, st_ref, *, sub):
    c = hq_ref.shape[0]
    n_sub = c // sub

    @pl.when(pl.program_id(1) == 0)
    def _():
        st_ref[...] = jnp.zeros_like(st_ref)

    ltri = _tri(c, lambda r, cc: cc <= r)
    t_i = lax.broadcasted_iota(jnp.int32, (c, c), 0)
    s_i = lax.broadcasted_iota(jnp.int32, (c, c), 1)
    pos = lax.broadcasted_iota(jnp.int32, (c, HG_D), 0)
    pos_sub = lax.broadcasted_iota(jnp.int32, (n_sub, sub, 1), 1)

    q_all, f_all, k_all = _hgrn_gates(hq_ref[...], hf_ref[...], lb_ref[...])
    p0, p1, p2 = _split3(jnp.log(f_all))
    g3 = _dot(ltri, jnp.concatenate([p0, p1, p2], axis=1))
    g_all = g3[:, :HG_W] + g3[:, HG_W:2 * HG_W] + g3[:, 2 * HG_W:]
    gate_all = _sigmoid(hg_ref[...])

    outs = []
    for h in range(N_HEADS):
        ws = slice(h * HG_D, (h + 1) * HG_D)
        q, k, g, v = q_all[:, ws], k_all[:, ws], g_all[:, ws], hi_ref[:, ws]
        st = st_ref[h]
        o = _dot_nt((q * jnp.exp(g)).astype(BF16), st.astype(BF16))

        q3, k3, g3d, v3 = (a.reshape(n_sub, sub, HG_D) for a in (q, k, g, v))
        o_diag = jnp.zeros((n_sub, sub, HG_D), F32)
        for s in range(sub):
            dec = jnp.exp(jnp.minimum(g3d - g3d[:, s:s + 1, :], 0.0))
            col = jnp.sum(q3 * k3[:, s:s + 1, :] * dec, axis=-1, keepdims=True)
            o_diag = o_diag + jnp.where(pos_sub >= s, col, 0.0) * v3[:, s:s + 1, :]
        o = o + o_diag.reshape(c, HG_D)

        a = jnp.zeros((c, c), F32)
        blk = 2 * sub
        while blk <= c:
            half = blk // 2
            gr = g.reshape(c // blk, blk, HG_D)[:, half - 1:half, :]
            e = jnp.exp(-jnp.abs(g.reshape(c // blk, blk, HG_D) - gr)).reshape(c, HG_D)
            late = (pos & (blk - 1)) >= half
            qs = jnp.where(late, q * e, 0.0).astype(BF16)
            ks = jnp.where(late, 0.0, k * e).astype(BF16)
            a_l = _dot_nt(qs, ks)
            sh = blk.bit_length() - 1
            a = a + (a_l if blk == c else jnp.where((t_i >> sh) == (s_i >> sh), a_l, 0.0))
            blk *= 2
        o = o + _dot(a.astype(BF16), v.astype(BF16))

        g_end = g[c - 1:c, :]
        k_end = (k * jnp.exp(g_end - g)).astype(BF16)
        st_ref[h] = st * jnp.exp(g_end) + _dot(v.T.astype(BF16), k_end)
        outs.append(_rms(o, g_ref[...]) * gate_all[:, ws])
    o_ref[...] = jnp.concatenate(outs, axis=1)

    @pl.when(pl.program_id(1) == pl.num_programs(1) - 1)
    def _():
        for h in range(N_HEADS):
            s_ref[0, h] = st_ref[h].T


def hgrn_prompt(p, lb_row, g_row, *, batch, seq, chunk, sub):
    n = p.shape[0]
    nc = seq // chunk
    col = lambda c0: (lambda b, t: (b * nc + t, c0 // HG_W))
    kern = functools.partial(_hgrn_kernel, sub=sub)
    return pl.pallas_call(
        kern,
        grid=(batch, nc),
        in_specs=[pl.BlockSpec((chunk, HG_W), col(C_HGQ)),
                  pl.BlockSpec((chunk, HG_W), col(C_HGF)),
                  pl.BlockSpec((chunk, HG_W), col(C_HGI)),
                  pl.BlockSpec((chunk, HG_W), col(C_HGG)),
                  pl.BlockSpec((1, HG_W), lambda b, t: (0, 0)),
                  pl.BlockSpec((1, HG_D), lambda b, t: (0, 0))],
        out_specs=[pl.BlockSpec((chunk, HG_W), lambda b, t: (b * nc + t, 0)),
                   pl.BlockSpec((1, N_HEADS, HG_D, HG_D), lambda b, t: (b, 0, 0, 0))],
        out_shape=[jax.ShapeDtypeStruct((n, HG_W), F32),
                   jax.ShapeDtypeStruct((batch, N_HEADS, HG_D, HG_D), F32)],
        scratch_shapes=[pltpu.VMEM((N_HEADS, HG_D, HG_D), F32)],
        compiler_params=_cparams("parallel", "arbitrary"),
        name="hgrn_prompt",
    )(p, p, p, p, lb_row, g_row)


def _out_proj_kernel(x_ref, a_ref, b_ref, c_ref, w_ref, o_ref, *, precise):
    mix = jnp.concatenate([a_ref[...], b_ref[...], c_ref[...]], axis=1)
    o_ref[...] = x_ref[...] + _mm(mix, w_ref[...], precise)


def out_proj(x, sb_o, fx_o, hg_o, w, *, tm):
    n, d = x.shape
    row = lambda i: (i, 0)
    return pl.pallas_call(
        functools.partial(_out_proj_kernel, precise=w.dtype == F32),
        grid=(n // tm,),
        in_specs=[pl.BlockSpec((tm, d), row),
                  pl.BlockSpec((tm, ATT_W), row),
                  pl.BlockSpec((tm, ATT_W), row),
                  pl.BlockSpec((tm, HG_W), row),
                  pl.BlockSpec(w.shape, lambda i: (0, 0))],
        out_specs=pl.BlockSpec((tm, d), row),
        out_shape=jax.ShapeDtypeStruct((n, d), F32),
        compiler_params=_cparams("parallel"),
        name="out_proj",
    )(x, sb_o, fx_o, hg_o, w)


def _ffn_kernel(h_ref, g_ref, wg_ref, wu_ref, wd_ref, o_ref, hn_ref, acc_ref, *, precise):
    j = pl.program_id(1)

    @pl.when(j == 0)
    def _():
        hn_ref[...] = _rms(h_ref[...], g_ref[...]).astype(hn_ref.dtype)
        acc_ref[...] = jnp.zeros_like(acc_ref)

    hn = hn_ref[...]
    gate = _mm(hn, wg_ref[0], precise)
    up = _mm(hn, wu_ref[0], precise)
    acc_ref[...] += _mm(gate * _sigmoid(gate) * up, wd_ref[0], precise)

    @pl.when(j == pl.num_programs(1) - 1)
    def _():
        o_ref[...] = h_ref[...] + acc_ref[...]


def ffn_dense(h, g_row, wg, wu, wd, layer, *, tm, tf):
    n, d = h.shape
    ff = wg.shape[2]
    precise = wg.dtype == F32
    return pl.pallas_call(
        functools.partial(_ffn_kernel, precise=precise),
        grid=(n // tm, ff // tf),
        in_specs=[pl.BlockSpec((tm, d), lambda i, j: (i, 0)),
                  pl.BlockSpec((1, d), lambda i, j: (0, 0)),
                  pl.BlockSpec((1, d, tf), lambda i, j: (layer, 0, j)),
                  pl.BlockSpec((1, d, tf), lambda i, j: (layer, 0, j)),
                  pl.BlockSpec((1, tf, d), lambda i, j: (layer, j, 0))],
        out_specs=pl.BlockSpec((tm, d), lambda i, j: (i, 0)),
        out_shape=jax.ShapeDtypeStruct((n, d), F32),
        scratch_shapes=[pltpu.VMEM((tm, d), F32 if precise else BF16), pltpu.VMEM((tm, d), F32)],
        compiler_params=_cparams("parallel", "arbitrary"),
        name="ffn_dense",
    )(h, g_row, wg, wu, wd)


def _router_kernel(h_ref, g_ref, w_ref, b_ref, o_ref, *, n_experts):
    hn = _rms(h_ref[...], g_ref[...])
    xh, xl = _split2(hn)
    wh, wl = _split2(w_ref[...])
    logits = _dot(xh, wh) + _dot(xl, wh) + _dot(xh, wl) + b_ref[...]
    lane = lax.broadcasted_iota(jnp.int32, logits.shape, 1).astype(F32)
    logits = jnp.where(lane < n_experts, logits, NEG_BIG)
    m1 = jnp.max(logits, axis=1, keepdims=True)
    i1 = jnp.min(jnp.where(logits == m1, lane, float(LANES)), axis=1, keepdims=True)
    first = lane == i1
    rest = jnp.where(first, NEG_BIG, logits)
    m2 = jnp.max(rest, axis=1, keepdims=True)
    i2 = jnp.min(jnp.where(rest == m2, lane, float(LANES)), axis=1, keepdims=True)
    second = lane == i2
    e = jnp.exp(m2 - m1)
    w1 = 1.0 / (1.0 + e)
    o_ref[...] = jnp.where(first, w1, 0.0) + jnp.where(second, e * w1, 0.0)


def moe_router(h, g_row, w_pad, b_pad, *, tm, n_experts):
    n, d = h.shape
    kern = functools.partial(_router_kernel, n_experts=n_experts)
    return pl.pallas_call(
        kern,
        grid=(n // tm,),
        in_specs=[pl.BlockSpec((tm, d), lambda i: (i, 0)),
                  pl.BlockSpec((1, d), lambda i: (0, 0)),
                  pl.BlockSpec((d, LANES), lambda i: (0, 0)),
                  pl.BlockSpec((1, LANES), lambda i: (0, 0))],
        out_specs=pl.BlockSpec((tm, LANES), lambda i: (i, 0)),
        out_shape=jax.ShapeDtypeStruct((n, LANES), F32),
        compiler_params=_cparams("parallel"),
        name="moe_router",
    )(h, g_row, w_pad, b_pad)


def _moe_kernel(h_ref, g_ref, gates_ref, wg_ref, wu_ref, wd_ref, o_ref, hn_ref, acc_ref, *, precise):
    e = pl.program_id(1)

    @pl.when(e == 0)
    def _():
        hn_ref[...] = _rms(h_ref[...], g_ref[...]).astype(hn_ref.dtype)
        acc_ref[...] = jnp.zeros_like(acc_ref)

    gates = gates_ref[...]
    lane = lax.broadcasted_iota(jnp.int32, gates.shape, 1)
    ge = jnp.sum(jnp.where(lane == e, gates, 0.0), axis=1, keepdims=True)
    hn = hn_ref[...]
    gate = _mm(hn, wg_ref[0, 0], precise)
    up = _mm(hn, wu_ref[0, 0], precise)
    acc_ref[...] += _mm(gate * _sigmoid(gate) * up * ge, wd_ref[0, 0], precise)

    @pl.when(e == pl.num_programs(1) - 1)
    def _():
        o_ref[...] = h_ref[...] + acc_ref[...]


def moe_ffn(h, g_row, gates, wg, wu, wd, layer, *, tm):
    n, d = h.shape
    _, n_e, _, ffe = wg.shape
    precise = wg.dtype == F32
    return pl.pallas_call(
        functools.partial(_moe_kernel, precise=precise),
        grid=(n // tm, n_e),
        in_specs=[pl.BlockSpec((tm, d), lambda i, e: (i, 0)),
                  pl.BlockSpec((1, d), lambda i, e: (0, 0)),
                  pl.BlockSpec((tm, LANES), lambda i, e: (i, 0)),
                  pl.BlockSpec((1, 1, d, ffe), lambda i, e: (layer, e, 0, 0)),
                  pl.BlockSpec((1, 1, d, ffe), lambda i, e: (layer, e, 0, 0)),
                  pl.BlockSpec((1, 1, ffe, d), lambda i, e: (layer, e, 0, 0))],
        out_specs=pl.BlockSpec((tm, d), lambda i, e: (i, 0)),
        out_shape=jax.ShapeDtypeStruct((n, d), F32),
        scratch_shapes=[pltpu.VMEM((tm, d), F32 if precise else BF16), pltpu.VMEM((tm, d), F32)],
        compiler_params=_cparams("parallel", "arbitrary"),
        name="moe_ffn",
    )(h, g_row, gates, wg, wu, wd)


def _head_rows(width):
    r = lax.broadcasted_iota(jnp.int32, (8, width), 0)
    c = lax.broadcasted_iota(jnp.int32, (8, width), 1)
    return (c // HEAD_DIM) == r


def _lane_to_rows(row_vec):
    r = lax.broadcasted_iota(jnp.int32, (8, LANES), 0)
    c = lax.broadcasted_iota(jnp.int32, (8, LANES), 1)
    return jnp.sum(jnp.where(r == c, row_vec, 0.0), axis=1, keepdims=True)


def _finish_decode(acc, own, g_ref, o_ref):
    o = jnp.sum(jnp.where(own, acc, 0.0), axis=0, keepdims=True)
    o_ref[0] = _rms(o, g_ref[...])


def _sb_decode_kernel(pt_ref, q_ref, g_ref, *rest, pps):
    k_refs, v_refs = rest[:pps], rest[pps:2 * pps]
    o_ref, acc_ref, r_ref = rest[2 * pps:]
    c = pl.program_id(1)

    @pl.when(c == 0)
    def _():
        acc_ref[...] = jnp.zeros_like(acc_ref)
        r_ref[...] = jnp.zeros_like(r_ref)

    own = _head_rows(ATT_W)
    qb = jnp.where(own, q_ref[0] * (HEAD_DIM ** -0.5), 0.0)
    u = _tri(LANES, lambda r, cc: r >= cc)
    order = list(reversed(range(pps)))
    zs = [_dot3(qb, k_refs[i][0]) for i in order]
    css = [_dot_ones(_softplus(z), u) for z in zs]
    acc = acc_ref[...]
    run = r_ref[...]
    for i, z, cs in zip(order, zs, css):
        acc = acc + _dot3_nt(jnp.exp(z - cs - run), v_refs[i][0])
        run = run + cs[:, 0:1]
    acc_ref[...] = acc
    r_ref[...] = run

    @pl.when(c == pl.num_programs(1) - 1)
    def _():
        _finish_decode(acc, own, g_ref, o_ref)


def _page_specs(n_chunks, pps, page_off, block):
    def spec(i):
        def index(b, c, pt):
            return (page_off + pt[b, (n_chunks - 1 - c) * pps + i],) + (0,) * (len(block) - 1)
        return pl.BlockSpec(block, index)
    return [spec(i) for i in range(pps)]


def sb_decode(q3, g_row, cache_k, cache_v, page_table, *, page_off, pps):
    nb = q3.shape[0]
    n_pages = page_table.shape[1]
    n_chunks = n_pages // pps
    page = cache_k.shape[2]
    kern = functools.partial(_sb_decode_kernel, pps=pps)
    pages = _page_specs(n_chunks, pps, page_off, (1, ATT_W, page))
    grid_spec = pltpu.PrefetchScalarGridSpec(
        num_scalar_prefetch=1,
        grid=(nb, n_chunks),
        in_specs=[pl.BlockSpec((1, 1, ATT_W), lambda b, c, pt: (b, 0, 0)),
                  pl.BlockSpec((1, ATT_W), lambda b, c, pt: (0, 0))] + pages + pages,
        out_specs=pl.BlockSpec((1, 1, ATT_W), lambda b, c, pt: (b, 0, 0)),
        scratch_shapes=[pltpu.VMEM((8, ATT_W), F32), pltpu.VMEM((8, 1), F32)],
    )
    return pl.pallas_call(
        kern,
        grid_spec=grid_spec,
        out_shape=jax.ShapeDtypeStruct((nb, 1, ATT_W), F32),
        compiler_params=_cparams("parallel", "arbitrary"),
        name="sb_decode",
    )(page_table, q3, g_row, *([cache_k] * pps), *([cache_v] * pps))


def _fox_decode_kernel(pt_ref, q_ref, k_ref, v_ref, f_ref, gq_ref, gk_ref, fb_ref, g_ref, *rest, pps):
    k_refs, v_refs, f_refs = rest[:pps], rest[pps:2 * pps], rest[2 * pps:3 * pps]
    o_ref, kn_ref, lf_ref, acc_ref, m_ref, l_ref, r_ref, qb_ref = rest[3 * pps:]
    c = pl.program_id(1)
    own = _head_rows(ATT_W)

    @pl.when(c == 0)
    def _():
        def head_norm(x_row, gain_row):
            xb = jnp.where(own, x_row, 0.0)
            ms = jnp.sum(xb * xb, axis=1, keepdims=True) * (1.0 / HEAD_DIM)
            return xb * lax.rsqrt(ms + EPS) * gain_row

        qn = head_norm(q_ref[0], gq_ref[...]) * (HEAD_DIM ** -0.5)
        kn = head_norm(k_ref[0], gk_ref[...])
        kn_ref[0] = jnp.sum(kn, axis=0, keepdims=True)
        fx = f_ref[0] + fb_ref[...]
        lf = jnp.minimum(fx, 0.0) - jnp.log(1.0 + jnp.exp(-jnp.abs(fx)))
        lf_ref[0] = lf
        qb_ref[...] = qn
        m_ref[...] = jnp.sum(qn * kn, axis=1, keepdims=True)
        l_ref[...] = jnp.ones_like(l_ref)
        acc_ref[...] = jnp.broadcast_to(v_ref[0], acc_ref.shape)
        r_ref[...] = _lane_to_rows(lf)

    qb = qb_ref[...]
    u = _tri(LANES, lambda r, cc: r >= cc)
    m_old, run = m_ref[...], r_ref[...]
    pad = jnp.zeros((8 - N_HEADS, LANES), F32)
    order = list(reversed(range(pps)))
    zs = [_dot3(qb, k_refs[i][0]) for i in order]
    lfs =[jnp.concatenate([f_refs[i][0], pad], axis=0) for i in order]
    css = [_dot_ones(lf, u) for lf in lfs]
    ss = []
    for z, lf, cs in zip(zs, lfs, css):
        ss.append(z + (cs - lf) + run)
        run = run + cs[:, 0:1]
    m_new = m_old
    for s in ss:
        m_new = jnp.maximum(m_new, jnp.max(s, axis=1, keepdims=True))
    alpha = jnp.exp(m_old - m_new)
    l = alpha * l_ref[...]
    acc = alpha * acc_ref[...]
    for i, s in zip(order, ss):
        pr = jnp.exp(s - m_new)
        l = l + jnp.sum(pr, axis=1, keepdims=True)
        acc = acc + _dot3_nt(pr, v_refs[i][0])
    acc_ref[...], m_ref[...], l_ref[...], r_ref[...] = acc, m_new, l, run

    @pl.when(c == pl.num_programs(1) - 1)
    def _():
        _finish_decode(acc / l, own, g_ref, o_ref)


def fox_decode(q3, k3, v3, f3, gq_row, gk_row, fb_row, g_row, cache_k, cache_v, cache_ft, page_table,
               *, page_off, pps):
    nb = q3.shape[0]
    n_pages = page_table.shape[1]
    n_chunks = n_pages // pps
    page = cache_k.shape[2]
    kern = functools.partial(_fox_decode_kernel, pps=pps)
    pages = _page_specs(n_chunks, pps, page_off, (1, ATT_W, page))
    f_pages = _page_specs(n_chunks, pps, page_off, (1, N_HEADS, page))
    tok = lambda w: pl.BlockSpec((1, 1, w), lambda b, c, pt: (b, 0, 0))
    par = lambda w: pl.BlockSpec((1, w), lambda b, c, pt: (0, 0))
    grid_spec = pltpu.PrefetchScalarGridSpec(
        num_scalar_prefetch=1,
        grid=(nb, n_chunks),
        in_specs=[tok(ATT_W), tok(ATT_W), tok(ATT_W), tok(LANES),
                  par(ATT_W), par(ATT_W), par(LANES), par(ATT_W)] + pages + pages + f_pages,
        out_specs=[tok(ATT_W), tok(ATT_W), tok(LANES)],
        scratch_shapes=[pltpu.VMEM((8, ATT_W), F32), pltpu.VMEM((8, 1), F32), pltpu.VMEM((8, 1), F32),
                        pltpu.VMEM((8, 1), F32), pltpu.VMEM((8, ATT_W), F32)],
    )
    return pl.pallas_call(
        kern,
        grid_spec=grid_spec,
        out_shape=[jax.ShapeDtypeStruct((nb, 1, ATT_W), F32),
                   jax.ShapeDtypeStruct((nb, 1, ATT_W), F32),
                   jax.ShapeDtypeStruct((nb, 1, LANES), F32)],
        compiler_params=_cparams("parallel", "arbitrary"),
        name="fox_decode",
    )(page_table, q3, k3, v3, f3, gq_row, gk_row, fb_row, g_row,
      *([cache_k] * pps), *([cache_v] * pps), *([cache_ft] * pps))


def _row_to_col(row_vec):
    r = lax.broadcasted_iota(jnp.int32, (HG_D, HG_D), 0)
    c = lax.broadcasted_iota(jnp.int32, (HG_D, HG_D), 1)
    return jnp.sum(jnp.where(r == c, row_vec, 0.0), axis=1, keepdims=True)


def _hgrn_decode_kernel(hq_ref, hf_ref, hi_ref, hg_ref, lb_ref, g_ref, s_ref, o_ref, sn_ref):
    outs = []
    for h in range(N_HEADS):
        ws = slice(h * HG_D, (h + 1) * HG_D)
        q, f, k = _hgrn_gates(hq_ref[0][:, ws], hf_ref[0][:, ws], lb_ref[:, ws])
        s_new = s_ref[0, h] * _row_to_col(f) + _row_to_col(k) * hi_ref[0][:, ws]
        sn_ref[0, h] = s_new
        o = jnp.sum(_row_to_col(q) * s_new, axis=0, keepdims=True)
        outs.append(_rms(o, g_ref[...]) * _sigmoid(hg_ref[0][:, ws]))
    o_ref[0] = jnp.concatenate(outs, axis=1)


def hgrn_decode(hq3, hf3, hi3, hg3, lb_row, g_row, state):
    nb = hq3.shape[0]
    tok = pl.BlockSpec((1, 1, HG_W), lambda b: (b, 0, 0))
    st = pl.BlockSpec((1, N_HEADS, HG_D, HG_D), lambda b: (b, 0, 0, 0))
    return pl.pallas_call(
        _hgrn_decode_kernel,
        grid=(nb,),
        in_specs=[tok, tok, tok, tok,
                  pl.BlockSpec((1, HG_W), lambda b: (0, 0)),
                  pl.BlockSpec((1, HG_D), lambda b: (0, 0)), st],
        out_specs=[tok, st],
        out_shape=[jax.ShapeDtypeStruct((nb, 1, HG_W), F32),
                   jax.ShapeDtypeStruct(state.shape, F32)],
        compiler_params=_cparams("parallel"),
        name="hgrn_decode",
    )(hq3, hf3, hi3, hg3, lb_row, g_row, state)


def _arrange_w_in(w_in_l):
    d = w_in_l.shape[0]
    a = 2 * 3 * ATT_W
    main = jnp.concatenate([w_in_l[:, :a], w_in_l[:, a + N_HEADS:]], axis=1)
    fcols = w_in_l[:, a:a + N_HEADS]
    pad = jnp.zeros((d, N_IN_PAD - C_FXF - N_HEADS), w_in_l.dtype)
    return jnp.concatenate([main, fcols, pad], axis=1).astype(F32)


def _seg_matrix():
    r = lax.broadcasted_iota(jnp.int32, (ATT_W, ATT_W), 0) // HEAD_DIM
    c = lax.broadcasted_iota(jnp.int32, (ATT_W, ATT_W), 1) // HEAD_DIM
    return jnp.where(r == c, 1.0, 0.0).astype(BF16)


def _row(v, width=None):
    v = v.reshape(1, -1).astype(F32)
    if width is not None and v.shape[1] < width:
        v = jnp.pad(v, ((0, 0), (0, width - v.shape[1])))
    return v


def kernel(x_prompt, x_sample, cache_sb_k, cache_sb_v, cache_fox_k, cache_fox_v, cache_fox_logf, state_hgrn,
           page_table, w_in, w_out, norm_mix_g, norm_ffn_g, fox_q_norm_g, fox_k_norm_g, fox_f_bias, sb_out_g,
           fox_out_g, hgrn_out_g, hgrn_lb_logits, ffn_w_gate, ffn_w_up, ffn_w_down, moe_router_w,
           moe_router_b, moe_w_gate, moe_w_up, moe_w_down):
    batch, seq, d_model = x_prompt.shape
    dec_b = x_sample.shape[0]
    depth = w_in.shape[0]
    n_pool, page = cache_sb_k.shape[1], cache_sb_k.shape[2]
    n_experts = moe_router_w.shape[-1]

    lb_all = lower_bounds(hgrn_lb_logits.astype(F32))
    seg = _seg_matrix()
    def pages_t(c):
        return jnp.transpose(c, (0, 1, 3, 4, 2)).reshape(depth * n_pool, ATT_W, page)

    sbk_c, sbv_c, fxk_c, fxv_c = (pages_t(c) for c in (cache_sb_k, cache_sb_v, cache_fox_k, cache_fox_v))
    fxf_c = jnp.swapaxes(cache_fox_logf, 2, 3).reshape(depth * n_pool, N_HEADS, page)

    xp = x_prompt.reshape(batch * seq, d_model)
    xs = x_sample.reshape(dec_b, d_model)
    p_out = [[] for _ in range(6)]
    s_out = [[] for _ in range(6)]
    kv_stacks, fox_stacks = None, None

    for l in range(depth):
        w_in_f = _arrange_w_in(w_in[l])
        w_in_l = w_in_f.astype(BF16)
        w_out_f = w_out[l].astype(F32)
        w_out_l = w_out_f.astype(BF16)
        g_mix, g_ffn = _row(norm_mix_g[l]), _row(norm_ffn_g[l])
        gq = _row(jnp.tile(fox_q_norm_g[l], N_HEADS))
        gk = _row(jnp.tile(fox_k_norm_g[l], N_HEADS))
        fb = _row(fox_f_bias[l], LANES)
        g_sb, g_fx, g_hg = _row(sb_out_g[l]), _row(fox_out_g[l]), _row(hgrn_out_g[l])
        lb_row = lb_all[l:l + 1]

        p, kv_stacks = in_proj_stacked(xp, g_mix, w_in_l, kv_stacks, l, depth, batch, tm=512, tn=1280)
        kn_stack, lf_stack, qa, ka = fox_prep(p, gq, gk, fb, seg, fox_stacks, l, depth,
                                              batch=batch, seq=seq, tb=256)
        fox_stacks = (kn_stack, lf_stack)
        sb_o = sb_attention(p, g_sb, batch=batch, seq=seq, tq=512, tk=256)
        fx_o = fox_attention(qa, ka, p, g_fx, batch=batch, seq=seq, tq=256)
        hg_o, s_fin = hgrn_prompt(p, lb_row, g_hg, batch=batch, seq=seq, chunk=128, sub=8)
        hp = out_proj(xp, sb_o, fx_o, hg_o, w_out_l, tm=512)
        p_out[5].append(s_fin)

        ps = in_proj(xs, g_mix, w_in_f, tm=dec_b, tn=1280)
        tok = lambda c0, w: ps[:, c0:c0 + w].reshape(dec_b, 1, w)
        off = l * n_pool
        sb_os = sb_decode(tok(C_SBQ, ATT_W), g_sb, sbk_c, sbv_c, page_table, page_off=off, pps=32)
        fx_os, fxk_s, lf_s = fox_decode(tok(C_FXQ, ATT_W), tok(C_FXK, ATT_W), tok(C_FXV, ATT_W),
                                        tok(C_FXF, LANES), gq, gk, fb, g_fx, fxk_c, fxv_c, fxf_c,
                                        page_table, page_off=off, pps=32)
        hg_os, s_new = hgrn_decode(tok(C_HGQ, HG_W), tok(C_HGF, HG_W), tok(C_HGI, HG_W), tok(C_HGG, HG_W),
                                   lb_row, g_hg, state_hgrn[l].astype(F32))
        hs = out_proj(xs, sb_os.reshape(dec_b, ATT_W), fx_os.reshape(dec_b, ATT_W),
                      hg_os.reshape(dec_b, HG_W), w_out_f, tm=dec_b)
        s_out[0].append(ps[:, C_SBK:C_SBK + ATT_W])
        s_out[1].append(ps[:, C_SBV:C_SBV + ATT_W])
        s_out[2].append(fxk_s.reshape(dec_b, ATT_W))
        s_out[3].append(ps[:, C_FXV:C_FXV + ATT_W])
        s_out[4].append(lf_s.reshape(dec_b, LANES)[:, :N_HEADS])
        s_out[5].append(s_new)

        if l % 2 == 0:
            wf = [w.astype(F32) for w in (ffn_w_gate, ffn_w_up, ffn_w_down)]
            wg, wu, wd = (w[l // 2][None].astype(BF16) for w in wf)
            xp = ffn_dense(hp, g_ffn, wg, wu, wd, 0, tm=1024, tf=256)
            xs = ffn_dense(hs, g_ffn, *wf, l // 2, tm=dec_b, tf=256)
        else:
            wf = [w.astype(F32) for w in (moe_w_gate, moe_w_up, moe_w_down)]
            wg, wu, wd = (w[l // 2][None].astype(BF16) for w in wf)
            rw = jnp.pad(moe_router_w[l // 2].astype(F32), ((0, 0), (0, LANES - n_experts)))
            rb = _row(moe_router_b[l // 2], LANES)
            gates_p = moe_router(hp, g_ffn, rw, rb, tm=512, n_experts=n_experts)
            xp = moe_ffn(hp, g_ffn, gates_p, wg, wu, wd, 0, tm=512)
            gates_s = moe_router(hs, g_ffn, rw, rb, tm=dec_b, n_experts=n_experts)
            xs = moe_ffn(hs, g_ffn, gates_s, *wf, l // 2, tm=dec_b)

    def stack(parts, shape):
        return jnp.stack(parts).reshape((depth,) + shape)

    hd = (N_HEADS, HEAD_DIM)
    y_prompt = xp.reshape(batch, seq, d_model)
    y_sample = xs.reshape(dec_b, 1, d_model)
    sbk_p, sbv_p, fxv_p = kv_stacks
    kn_stack, lf_stack = fox_stacks
    outs_p = [jnp.transpose(a.reshape((depth, batch) + hd + (seq,)), (0, 1, 4, 2, 3))
              for a in (sbk_p, sbv_p, kn_stack, fxv_p)]
    outs_p.append(jnp.swapaxes(lf_stack, 2, 3))
    outs_p.append(stack(p_out[5], (batch, N_HEADS, HG_D, HG_D)).astype(state_hgrn.dtype))
    outs_s = [stack(s_out[i], (dec_b, 1) + hd) for i in range(4)]
    outs_s.append(stack(s_out[4], (dec_b, 1, N_HEADS)))
    outs_s.append(stack(s_out[5], (dec_b, N_HEADS, HG_D, HG_D)).astype(state_hgrn.dtype))
    return (y_prompt, y_sample, *outs_p, *outs_s)
```

```python
import functools

import jax
import jax.numpy as jnp
from jax import lax
from jax.experimental import pallas as pl
from jax.experimental.pallas import tpu as pltpu

F32 = jnp.float32
BF16 = jnp.bfloat16
EPS = 1e-6
NEG_BIG = -1e30

HEAD_DIM = 64
N_HEADS = 4
ATT_W = N_HEADS * HEAD_DIM
HG_D = 128
HG_W = N_HEADS * HG_D
LANES = 128
VMEM_LIMIT = 56 * 1024 * 1024

C_SBQ, C_SBK, C_SBV = 0, 256, 512
C_FXQ, C_FXK, C_FXV = 768, 1024, 1280
C_HGQ, C_HGF, C_HGI, C_HGG = 1536, 2048, 2560, 3072
C_FXF = 3584
N_IN_PAD = 3840


def _cparams(*sem):
    return pltpu.CompilerParams(dimension_semantics=sem, vmem_limit_bytes=VMEM_LIMIT)


def _rms(xf, g_row):
    ms = jnp.mean(xf * xf, axis=-1, keepdims=True)
    return xf * lax.rsqrt(ms + EPS) * g_row


def _softplus(z):
    return jnp.maximum(z, 0.0) + jnp.log(1.0 + jnp.exp(-jnp.abs(z)))


def _sigmoid(x):
    return 1.0 / (1.0 + jnp.exp(-x))


def _split2(x):
    hi = x.astype(BF16)
    lo = (x - hi.astype(F32)).astype(BF16)
    return hi, lo


def _split3(x):
    hi = x.astype(BF16)
    r = x - hi.astype(F32)
    mid = r.astype(BF16)
    lo = (r - mid.astype(F32)).astype(BF16)
    return hi, mid, lo


def _dot(a, b):
    return jnp.dot(a, b, preferred_element_type=F32)


def _dot_nt(a, b):
    return lax.dot_general(a, b, (((1,), (1,)), ((), ())), preferred_element_type=F32)


def _stack_split2(x):
    hi, lo = _split2(x)
    return hi, jnp.concatenate([hi, lo], axis=0)


def _dot3(x, w):
    m = x.shape[0]
    xh, x2 = _stack_split2(x)
    wh, wl = _split2(w)
    r = _dot(x2, wh)
    return r[:m] + r[m:] + _dot(xh, wl)


def _dot3_nt(x, w):
    m = x.shape[0]
    xh, x2 = _stack_split2(x)
    wh, wl = _split2(w)
    r = _dot_nt(x2, wh)
    return r[:m] + r[m:] + _dot_nt(xh, wl)


def _dot_ones(x, ones_bf16):
    m = x.shape[0]
    r = _dot(jnp.concatenate(_split3(x), axis=0), ones_bf16)
    return r[:m] + r[m:2 * m] + r[2 * m:]


def _mm(x, w, precise):
    return _dot3(x, w) if precise else _dot(x.astype(BF16), w)


def _tri(n, rel):
    r = lax.broadcasted_iota(jnp.int32, (n, n), 0)
    c = lax.broadcasted_iota(jnp.int32, (n, n), 1)
    return jnp.where(rel(r, c), 1.0, 0.0).astype(BF16)


def _lower_bounds_kernel(x_ref, o_ref):
    x = x_ref[...]
    depth = x.shape[0]
    m = jnp.max(x, axis=0, keepdims=True)
    e = jnp.exp(x - m)
    sm = e / jnp.sum(e, axis=0, keepdims=True)
    run = jnp.zeros_like(sm[0:1])
    rows = []
    for l in range(depth):
        run = run + sm[l:l + 1]
        rows.append(run - sm[0:1])
    o_ref[...] = jnp.concatenate(rows, axis=0)


def lower_bounds(logits):
    return pl.pallas_call(
        _lower_bounds_kernel,
        out_shape=jax.ShapeDtypeStruct(logits.shape, F32),
        name="hgrn_lower_bounds",
    )(logits)


def _in_proj_kernel(x_ref, g_ref, w_ref, o_ref, xn_ref, *, precise):
    @pl.when(pl.program_id(1) == 0)
    def _():
        xn_ref[...] = _rms(x_ref[...], g_ref[...]).astype(xn_ref.dtype)

    o_ref[...] = _mm(xn_ref[...], w_ref[...], precise)


def in_proj(x, g_row, w, *, tm, tn):
    n, d = x.shape
    n_out = w.shape[1]
    precise = w.dtype == F32
    return pl.pallas_call(
        functools.partial(_in_proj_kernel, precise=precise),
        grid=(n // tm, n_out // tn),
        in_specs=[pl.BlockSpec((tm, d), lambda i, j: (i, 0)),
                  pl.BlockSpec((1, d), lambda i, j: (0, 0)),
                  pl.BlockSpec((d, tn), lambda i, j: (0, j))],
        out_specs=pl.BlockSpec((tm, tn), lambda i, j: (i, j)),
        out_shape=jax.ShapeDtypeStruct((n, n_out), F32),
        scratch_shapes=[pltpu.VMEM((tm, d), F32 if precise else BF16)],
        compiler_params=_cparams("parallel", "arbitrary"),
        name="norm_in_proj",
    )(x, g_row, w)


STACKED_COLS = (C_SBK, C_SBV, C_FXV)


def _in_proj_stack_kernel(x_ref, g_ref, w_ref, *rest, tn, n_alias):
    o_ref, s_refs, xn_ref = rest[n_alias], rest[n_alias + 1:-1], rest[-1]
    j = pl.program_id(1)

    @pl.when(j == 0)
    def _():
        xn_ref[...] = _rms(x_ref[...], g_ref[...]).astype(BF16)

    res = _dot(xn_ref[...], w_ref[...])
    o_ref[...] = res
    for c0, s_ref in zip(STACKED_COLS, s_refs):
        @pl.when(j == c0 // tn)
        def _(c0=c0, s_ref=s_ref):
            s_ref[0, 0] = res[:, c0 % tn:c0 % tn + ATT_W].T


def in_proj_stacked(x, g_row, w_bf16, stacks, layer, depth, batch, *, tm, tn):
    n, d = x.shape
    n_out = w_bf16.shape[1]
    seq = n // batch
    tiles = seq // tm
    assert all(c0 % tn + ATT_W <= tn for c0 in STACKED_COLS) and seq % tm == 0
    n_alias = 0 if stacks is None else len(STACKED_COLS)
    stack_sds = jax.ShapeDtypeStruct((depth, batch, ATT_W, seq), F32)
    outs = pl.pallas_call(
        functools.partial(_in_proj_stack_kernel, tn=tn, n_alias=n_alias),
        grid=(n // tm, n_out // tn),
        in_specs=[pl.BlockSpec((tm, d), lambda i, j: (i, 0)),
                  pl.BlockSpec((1, d), lambda i, j: (0, 0)),
                  pl.BlockSpec((d, tn), lambda i, j: (0, j))]
                 + [pl.BlockSpec(memory_space=pl.ANY)] * n_alias,
        out_specs=[pl.BlockSpec((tm, tn), lambda i, j: (i, j))]
                  + [pl.BlockSpec((1, 1, ATT_W, tm), lambda i, j: (layer, i // tiles, 0, i % tiles))]
                  * len(STACKED_COLS),
        out_shape=[jax.ShapeDtypeStruct((n, n_out), F32)] + [stack_sds] * len(STACKED_COLS),
        input_output_aliases={3 + a: 1 + a for a in range(n_alias)},
        scratch_shapes=[pltpu.VMEM((tm, d), BF16)],
        compiler_params=_cparams("parallel", "arbitrary"),
        name="norm_in_proj_stacked",
    )(x, g_row, w_bf16, *(stacks or ()))
    return outs[0], tuple(outs[1:])


def _seg_mean_sq(x, seg_ref):
    hi, lo = _split2(x * x)
    return (_dot(hi, seg_ref[...]) + _dot(lo, seg_ref[...])) * (1.0 / HEAD_DIM)


def _fox_prep_kernel(q_ref, k_ref, f_ref, gq_ref, gk_ref, fb_ref, seg_ref, *rest):
    kn_ref, lf_ref, qa_ref, ka_ref, carry_ref = rest[-5:]
    tb = q_ref.shape[0]

    @pl.when(pl.program_id(1) == 0)
    def _():
        carry_ref[...] = jnp.zeros_like(carry_ref)

    xq = q_ref[...]
    xk = k_ref[...]
    qn = xq * lax.rsqrt(_seg_mean_sq(xq, seg_ref) + EPS) * gq_ref[...]
    kn = xk * lax.rsqrt(_seg_mean_sq(xk, seg_ref) + EPS) * gk_ref[...]
    kn_ref[0, 0] = kn.T

    fx = f_ref[...] + fb_ref[...]
    lf = jnp.minimum(fx, 0.0) - jnp.log(1.0 + jnp.exp(-jnp.abs(fx)))
    lf_ref[0, 0] = lf.T[:N_HEADS]

    ltri = _tri(tb, lambda r, c: c <= r)
    p0, p1, p2 = _split3(lf)
    cum = _dot(ltri, p0) + _dot(ltri, p1) + _dot(ltri, p2) + carry_ref[...]
    carry_ref[...] = cum[tb - 1:tb, :]

    c0, c1, c2 = (c.astype(F32) for c in _split3(cum))
    lane = lax.broadcasted_iota(jnp.int32, (tb, HEAD_DIM), 1)
    one = jnp.where(lane < 6, 1.0, 0.0)
    qs = qn * (HEAD_DIM ** -0.5)
    q_parts, k_parts = [], []
    for h in range(N_HEADS):
        a0, a1, a2 = c0[:, h:h + 1], c1[:, h:h + 1], c2[:, h:h + 1]
        ext_q = jnp.where(lane == 0, a0, jnp.where(lane == 1, a1, jnp.where(lane == 2, a2, one)))
        ext_k = jnp.where(lane == 3, -a0, jnp.where(lane == 4, -a1, jnp.where(lane == 5, -a2, one)))
        q_parts += [qs[:, h * HEAD_DIM:(h + 1) * HEAD_DIM], ext_q]
        k_parts += [kn[:, h * HEAD_DIM:(h + 1) * HEAD_DIM], ext_k]
    qa_ref[...] = jnp.concatenate(q_parts, axis=1).astype(BF16)
    ka_ref[...] = jnp.concatenate(k_parts, axis=1).astype(BF16)


def fox_prep(p, gq_row, gk_row, fb_row, seg, stacks, layer, depth, *, batch, seq, tb):
    n = p.shape[0]
    nb = seq // tb
    row = lambda b, t: b * nb + t
    aw = N_HEADS * LANES
    alias = () if stacks is None else tuple(stacks)
    return pl.pallas_call(
        _fox_prep_kernel,
        grid=(batch, nb),
        in_specs=[pl.BlockSpec((tb, ATT_W), lambda b, t: (row(b, t), C_FXQ // ATT_W)),
                  pl.BlockSpec((tb, ATT_W), lambda b, t: (row(b, t), C_FXK // ATT_W)),
                  pl.BlockSpec((tb, LANES), lambda b, t: (row(b, t), C_FXF // LANES)),
                  pl.BlockSpec((1, ATT_W), lambda b, t: (0, 0)),
                  pl.BlockSpec((1, ATT_W), lambda b, t: (0, 0)),
                  pl.BlockSpec((1, LANES), lambda b, t: (0, 0)),
                  pl.BlockSpec((ATT_W, ATT_W), lambda b, t: (0, 0))]
                 + [pl.BlockSpec(memory_space=pl.ANY)] * len(alias),
        out_specs=[pl.BlockSpec((1, 1, ATT_W, tb), lambda b, t: (layer, b, 0, t)),
                   pl.BlockSpec((1, 1, N_HEADS, tb), lambda b, t: (layer, b, 0, t)),
                   pl.BlockSpec((tb, aw), lambda b, t: (row(b, t), 0)),
                   pl.BlockSpec((tb, aw), lambda b, t: (row(b, t), 0))],
        out_shape=[jax.ShapeDtypeStruct((depth, batch, ATT_W, seq), F32),
                   jax.ShapeDtypeStruct((depth, batch, N_HEADS, seq), F32),
                   jax.ShapeDtypeStruct((n, aw), BF16),
                   jax.ShapeDtypeStruct((n, aw), BF16)],
        input_output_aliases={7: 0, 8: 1} if alias else {},
        scratch_shapes=[pltpu.VMEM((1, LANES), F32)],
        compiler_params=_cparams("parallel", "arbitrary"),
        name="fox_prep",
    )(p, p, p, gq_row, gk_row, fb_row, seg, *alias)


def _store_v_transposed(v_ref, vt_ref, tk):
    for j in range(vt_ref.shape[0]):
        vt_ref[j] = v_ref[j * tk:(j + 1) * tk, :].T.astype(BF16)


def _finish_attention(ot, g_ref, o_ref):
    ms = jnp.mean(ot * ot, axis=0, keepdims=True)
    o_ref[...] = (ot * lax.rsqrt(ms + EPS)).T * g_ref[...]


def _sb_attn_kernel(q_ref, k_ref, v_ref, g_ref, o_ref, acc_ref, r_ref, *, tk):
    i = pl.program_id(1)
    tq = q_ref.shape[0]
    acc_ref[...] = jnp.zeros_like(acc_ref)
    r_ref[...] = jnp.zeros_like(r_ref)
    u = _tri(tk, lambda r, c: r >= c)
    n_diag = tq // tk
    qry_i = lax.broadcasted_iota(jnp.int32, (tq, tk), 0)
    key_i = lax.broadcasted_iota(jnp.int32, (tq, tk), 1)
    low_half = lax.broadcasted_iota(jnp.int32, (tq, LANES), 1) < HEAD_DIM
    qm = []
    for pair in range(N_HEADS // 2):
        qp = q_ref[:, pair * LANES:(pair + 1) * LANES] * (HEAD_DIM ** -0.5)
        qm.append(jnp.where(low_half, qp, 0.0).astype(BF16))
        qm.append(jnp.where(low_half, 0.0, qp).astype(BF16))

    def block(j, diag_off):
        row0 = pl.multiple_of(j * tk, tk)
        before = None if diag_off is None else (key_i + diag_off < qry_i)
        for pair in range(N_HEADS // 2):
            ls = slice(pair * LANES, (pair + 1) * LANES)
            kp = k_ref[pl.ds(row0, tk), ls].astype(BF16)
            vp = v_ref[pl.ds(row0, tk), ls].astype(BF16)
            res = []
            for h in (2 * pair, 2 * pair + 1):
                z = _dot_nt(qm[h], kp)
                sp = _softplus(z)
                if before is not None:
                    sp = jnp.where(before, sp, 0.0)
                cs = _dot(sp.astype(BF16), u)
                r_old = r_ref[h]
                a = jnp.exp(z - cs - r_old)
                if before is not None:
                    a = jnp.where(before, a, 0.0)
                res.append(_dot(a.astype(BF16), vp))
                r_ref[h] = r_old + cs[:, 0:1]
            acc_ref[:, ls] += jnp.where(low_half, res[0], res[1])

    for d in reversed(range(n_diag)):
        block(i * n_diag + d, d * tk)

    def body(jj, c):
        block(i * n_diag - 1 - jj, None)
        return c

    lax.fori_loop(0, i * n_diag, body, 0)
    o_ref[...] = _rms(acc_ref[...], g_ref[...])


def sb_attention(p, g_row, *, batch, seq, tq, tk):
    n = p.shape[0]
    nq = seq // tq
    assert tq % tk == 0
    kern = functools.partial(_sb_attn_kernel, tk=tk)
    return pl.pallas_call(
        kern,
        grid=(batch, nq),
        in_specs=[pl.BlockSpec((tq, ATT_W), lambda b, i: (b * nq + i, C_SBQ // ATT_W)),
                  pl.BlockSpec((seq, ATT_W), lambda b, i: (b, C_SBK // ATT_W)),
                  pl.BlockSpec((seq, ATT_W), lambda b, i: (b, C_SBV // ATT_W)),
                  pl.BlockSpec((1, ATT_W), lambda b, i: (0, 0))],
        out_specs=pl.BlockSpec((tq, ATT_W), lambda b, i: (b * nq + i, 0)),
        out_shape=jax.ShapeDtypeStruct((n, ATT_W), F32),
        scratch_shapes=[pltpu.VMEM((tq, ATT_W), F32),
                        pltpu.VMEM((N_HEADS, tq, 1), F32)],
        compiler_params=_cparams("parallel", "arbitrary"),
        name="sb_attention",
    )(p, p, p, g_row)


def _fox_attn_kernel(q_ref, k_ref, v_ref, g_ref, o_ref, vt_ref, acc_ref, m_ref, l_ref, *, tk):
    i = pl.program_id(1)
    tq = q_ref.shape[0]

    @pl.when(i == 0)
    def _():
        _store_v_transposed(v_ref, vt_ref, tk)

    acc_ref[...] = jnp.zeros_like(acc_ref)
    m_ref[...] = jnp.full_like(m_ref, NEG_BIG)
    l_ref[...] = jnp.zeros_like(l_ref)
    key_i = lax.broadcasted_iota(jnp.int32, (tk, tq), 0)
    qry_i = lax.broadcasted_iota(jnp.int32, (tk, tq), 1)
    causal = key_i <= qry_i

    def blocks(js, diagonal):
        sts = [[_dot_nt(k_ref[pl.ds(pl.multiple_of(j * tk, tk), tk), h * LANES:(h + 1) * LANES],
                        q_ref[:, h * LANES:(h + 1) * LANES]) for j in js] for h in range(N_HEADS)]
        for h in range(N_HEADS):
            hs = slice(h * HEAD_DIM, (h + 1) * HEAD_DIM)
            st_h = sts[h]
            if diagonal:
                st_h = [jnp.where(causal, st, NEG_BIG) for st in st_h]
            m_old = m_ref[h:h + 1, :]
            m_new = m_old
            for st in st_h:
                m_new = jnp.maximum(m_new, jnp.max(st, axis=0, keepdims=True))
            alpha = jnp.exp(m_old - m_new)
            l_new = alpha * l_ref[h:h + 1, :]
            acc_new = alpha * acc_ref[hs, :]
            for j, st in zip(js, st_h):
                pt = jnp.exp(st - m_new)
                l_new = l_new + jnp.sum(pt, axis=0, keepdims=True)
                acc_new = acc_new + _dot(vt_ref[j, hs, :], pt.astype(BF16))
            l_ref[h:h + 1, :] = l_new
            acc_ref[hs, :] = acc_new
            m_ref[h:h + 1, :] = m_new

    blocks([i], True)

    def body(jj, c):
        blocks([i - 1 - 2 * jj, i - 2 - 2 * jj], False)
        return c

    lax.fori_loop(0, i // 2, body, 0)

    @pl.when(i % 2 == 1)
    def _():
        blocks([0], False)
    parts = [acc_ref[h * HEAD_DIM:(h + 1) * HEAD_DIM, :] / l_ref[h:h + 1, :] for h in range(N_HEADS)]
    _finish_attention(jnp.concatenate(parts, axis=0), g_ref, o_ref)


def fox_attention(qa, ka, p, g_row, *, batch, seq, tq):
    n = p.shape[0]
    nq = seq // tq
    aw = N_HEADS * LANES
    kern = functools.partial(_fox_attn_kernel, tk=tq)
    return pl.pallas_call(
        kern,
        grid=(batch, nq),
        in_specs=[pl.BlockSpec((tq, aw), lambda b, i: (b * nq + i, 0)),
                  pl.BlockSpec((seq, aw), lambda b, i: (b, 0)),
                  pl.BlockSpec((seq, ATT_W), lambda b, i: (b, C_FXV // ATT_W)),
                  pl.BlockSpec((1, ATT_W), lambda b, i: (0, 0))],
        out_specs=pl.BlockSpec((tq, ATT_W), lambda b, i: (b * nq + i, 0)),
        out_shape=jax.ShapeDtypeStruct((n, ATT_W), F32),
        scratch_shapes=[pltpu.VMEM((nq, ATT_W, tq), BF16),
                        pltpu.VMEM((ATT_W, tq), F32),
                        pltpu.VMEM((8, tq), F32),
                        pltpu.VMEM((8, tq), F32)],
        compiler_params=_cparams("parallel", "arbitrary"),
        name="fox_attention",
    )(qa, ka, p, g_row)


def _hgrn_gates(hq, hf, lb):
    q = hq * _sigmoid(hq)
    f = lb + (1.0 - lb) * _sigmoid(hf)
    return q, f, 1.0 - f


def _hgrn_kernel(hq_ref, hf_ref, hi_ref, hg_ref, lb_ref, g_ref, o_ref, s_ref, st_ref, *, sub):
    c = hq_ref.shape[0]
    n_sub = c // sub

    @pl.when(pl.program_id(1) == 0)
    def _():
        st_ref[...] = jnp.zeros_like(st_ref)

    ltri = _tri(c, lambda r, cc: cc <= r)
    t_i = lax.broadcasted_iota(jnp.int32, (c, c), 0)
    s_i = lax.broadcasted_iota(jnp.int32, (c, c), 1)
    pos = lax.broadcasted_iota(jnp.int32, (c, HG_D), 0)
    pos_sub = lax.broadcasted_iota(jnp.int32, (n_sub, sub, 1), 1)

    q_all, f_all, k_all = _hgrn_gates(hq_ref[...], hf_ref[...], lb_ref[...])
    p0, p1, p2 = _split3(jnp.log(f_all))
    g3 = _dot(ltri, jnp.concatenate([p0, p1, p2], axis=1))
    g_all = g3[:, :HG_W] + g3[:, HG_W:2 * HG_W] + g3[:, 2 * HG_W:]
    gate_all = _sigmoid(hg_ref[...])

    outs = []
    for h in range(N_HEADS):
        ws = slice(h * HG_D, (h + 1) * HG_D)
        q, k, g, v = q_all[:, ws], k_all[:, ws], g_all[:, ws], hi_ref[:, ws]
        st = st_ref[h]
        o = _dot_nt((q * jnp.exp(g)).astype(BF16), st.astype(BF16))

        q3, k3, g3d, v3 = (a.reshape(n_sub, sub, HG_D) for a in (q, k, g, v))
        o_diag = jnp.zeros((n_sub, sub, HG_D), F32)
        for s in range(sub):
            dec = jnp.exp(jnp.minimum(g3d - g3d[:, s:s + 1, :], 0.0))
            col = jnp.sum(q3 * k3[:, s:s + 1, :] * dec, axis=-1, keepdims=True)
            o_diag = o_diag + jnp.where(pos_sub >= s, col, 0.0) * v3[:, s:s + 1, :]
        o = o + o_diag.reshape(c, HG_D)

        a = jnp.zeros((c, c), F32)
        blk = 2 * sub
        while blk <= c:
            half = blk // 2
            gr = g.reshape(c // blk, blk, HG_D)[:, half - 1:half, :]
            e = jnp.exp(-jnp.abs(g.reshape(c // blk, blk, HG_D) - gr)).reshape(c, HG_D)
            late = (pos & (blk - 1)) >= half
            qs = jnp.where(late, q * e, 0.0).astype(BF16)
            ks = jnp.where(late, 0.0, k * e).astype(BF16)
            a_l = _dot_nt(qs, ks)
            sh = blk.bit_length() - 1
            a = a + (a_l if blk == c else jnp.where((t_i >> sh) == (s_i >> sh), a_l, 0.0))
            blk *= 2
        o = o + _dot(a.astype(BF16), v.astype(BF16))

        g_end = g[c - 1:c, :]
        k_end = (k * jnp.exp(g_end - g)).astype(BF16)
        st_ref[h] = st * jnp.exp(g_end) + _dot(v.T.astype(BF16), k_end)
        outs.append(_rms(o, g_ref[...]) * gate_all[:, ws])
    o_ref[...] = jnp.concatenate(outs, axis=1)

    @pl.when(pl.program_id(1) == pl.num_programs(1) - 1)
    def _():
        for h in range(N_HEADS):
            s_ref[0, h] = st_ref[h].T


def hgrn_prompt(p, lb_row, g_row, *, batch, seq, chunk, sub):
    n = p.shape[0]
    nc = seq // chunk
    col = lambda c0: (lambda b, t: (b * nc + t, c0 // HG_W))
    kern = functools.partial(_hgrn_kernel, sub=sub)
    return pl.pallas_call(
        kern,
        grid=(batch, nc),
        in_specs=[pl.BlockSpec((chunk, HG_W), col(C_HGQ)),
                  pl.BlockSpec((chunk, HG_W), col(C_HGF)),
                  pl.BlockSpec((chunk, HG_W), col(C_HGI)),
                  pl.BlockSpec((chunk, HG_W), col(C_HGG)),
                  pl.BlockSpec((1, HG_W), lambda b, t: (0, 0)),
                  pl.BlockSpec((1, HG_D), lambda b, t: (0, 0))],
        out_specs=[pl.BlockSpec((chunk, HG_W), lambda b, t: (b * nc + t, 0)),
                   pl.BlockSpec((1, N_HEADS, HG_D, HG_D), lambda b, t: (b, 0, 0, 0))],
        out_shape=[jax.ShapeDtypeStruct((n, HG_W), F32),
                   jax.ShapeDtypeStruct((batch, N_HEADS, HG_D, HG_D), F32)],
        scratch_shapes=[pltpu.VMEM((N_HEADS, HG_D, HG_D), F32)],
        compiler_params=_cparams("parallel", "arbitrary"),
        name="hgrn_prompt",
    )(p, p, p, p, lb_row, g_row)


def _out_proj_kernel(x_ref, a_ref, b_ref, c_ref, w_ref, o_ref, *, precise):
    mix = jnp.concatenate([a_ref[...], b_ref[...], c_ref[...]], axis=1)
    o_ref[...] = x_ref[...] + _mm(mix, w_ref[...], precise)


def out_proj(x, sb_o, fx_o, hg_o, w, *, tm):
    n, d = x.shape
    row = lambda i: (i, 0)
    return pl.pallas_call(
        functools.partial(_out_proj_kernel, precise=w.dtype == F32),
        grid=(n // tm,),
        in_specs=[pl.BlockSpec((tm, d), row),
                  pl.BlockSpec((tm, ATT_W), row),
                  pl.BlockSpec((tm, ATT_W), row),
                  pl.BlockSpec((tm, HG_W), row),
                  pl.BlockSpec(w.shape, lambda i: (0, 0))],
        out_specs=pl.BlockSpec((tm, d), row),
        out_shape=jax.ShapeDtypeStruct((n, d), F32),
        compiler_params=_cparams("parallel"),
        name="out_proj",
    )(x, sb_o, fx_o, hg_o, w)


def _ffn_kernel(h_ref, g_ref, wg_ref, wu_ref, wd_ref, o_ref, hn_ref, acc_ref, *, precise):
    j = pl.program_id(1)

    @pl.when(j == 0)
    def _():
        hn_ref[...] = _rms(h_ref[...], g_ref[...]).astype(hn_ref.dtype)
        acc_ref[...] = jnp.zeros_like(acc_ref)

    hn = hn_ref[...]
    gate = _mm(hn, wg_ref[0], precise)
    up = _mm(hn, wu_ref[0], precise)
    acc_ref[...] += _mm(gate * _sigmoid(gate) * up, wd_ref[0], precise)

    @pl.when(j == pl.num_programs(1) - 1)
    def _():
        o_ref[...] = h_ref[...] + acc_ref[...]


def ffn_dense(h, g_row, wg, wu, wd, layer, *, tm, tf):
    n, d = h.shape
    ff = wg.shape[2]
    precise = wg.dtype == F32
    return pl.pallas_call(
        functools.partial(_ffn_kernel, precise=precise),
        grid=(n // tm, ff // tf),
        in_specs=[pl.BlockSpec((tm, d), lambda i, j: (i, 0)),
                  pl.BlockSpec((1, d), lambda i, j: (0, 0)),
                  pl.BlockSpec((1, d, tf), lambda i, j: (layer, 0, j)),
                  pl.BlockSpec((1, d, tf), lambda i, j: (layer, 0, j)),
                  pl.BlockSpec((1, tf, d), lambda i, j: (layer, j, 0))],
        out_specs=pl.BlockSpec((tm, d), lambda i, j: (i, 0)),
        out_shape=jax.ShapeDtypeStruct((n, d), F32),
        scratch_shapes=[pltpu.VMEM((tm, d), F32 if precise else BF16), pltpu.VMEM((tm, d), F32)],
        compiler_params=_cparams("parallel", "arbitrary"),
        name="ffn_dense",
    )(h, g_row, wg, wu, wd)


def _router_kernel(h_ref, g_ref, w_ref, b_ref, o_ref, *, n_experts):
    hn = _rms(h_ref[...], g_ref[...])
    xh, xl = _split2(hn)
    wh, wl = _split2(w_ref[...])
    logits = _dot(xh, wh) + _dot(xl, wh) + _dot(xh, wl) + b_ref[...]
    lane = lax.broadcasted_iota(jnp.int32, logits.shape, 1).astype(F32)
    logits = jnp.where(lane < n_experts, logits, NEG_BIG)
    m1 = jnp.max(logits, axis=1, keepdims=True)
    i1 = jnp.min(jnp.where(logits == m1, lane, float(LANES)), axis=1, keepdims=True)
    first = lane == i1
    rest = jnp.where(first, NEG_BIG, logits)
    m2 = jnp.max(rest, axis=1, keepdims=True)
    i2 = jnp.min(jnp.where(rest == m2, lane, float(LANES)), axis=1, keepdims=True)
    second = lane == i2
    e = jnp.exp(m2 - m1)
    w1 = 1.0 / (1.0 + e)
    o_ref[...] = jnp.where(first, w1, 0.0) + jnp.where(second, e * w1, 0.0)


def moe_router(h, g_row, w_pad, b_pad, *, tm, n_experts):
    n, d = h.shape
    kern = functools.partial(_router_kernel, n_experts=n_experts)
    return pl.pallas_call(
        kern,
        grid=(n // tm,),
        in_specs=[pl.BlockSpec((tm, d), lambda i: (i, 0)),
                  pl.BlockSpec((1, d), lambda i: (0, 0)),
                  pl.BlockSpec((d, LANES), lambda i: (0, 0)),
                  pl.BlockSpec((1, LANES), lambda i: (0, 0))],
        out_specs=pl.BlockSpec((tm, LANES), lambda i: (i, 0)),
        out_shape=jax.ShapeDtypeStruct((n, LANES), F32),
        compiler_params=_cparams("parallel"),
        name="moe_router",
    )(h, g_row, w_pad, b_pad)


def _moe_kernel(h_ref, g_ref, gates_ref, wg_ref, wu_ref, wd_ref, o_ref, hn_ref, acc_ref, *, precise):
    e = pl.program_id(1)

    @pl.when(e == 0)
    def _():
        hn_ref[...] = _rms(h_ref[...], g_ref[...]).astype(hn_ref.dtype)
        acc_ref[...] = jnp.zeros_like(acc_ref)

    gates = gates_ref[...]
    lane = lax.broadcasted_iota(jnp.int32, gates.shape, 1)
    ge = jnp.sum(jnp.where(lane == e, gates, 0.0), axis=1, keepdims=True)
    hn = hn_ref[...]
    gate = _mm(hn, wg_ref[0, 0], precise)
    up = _mm(hn, wu_ref[0, 0], precise)
    acc_ref[...] += _mm(gate * _sigmoid(gate) * up * ge, wd_ref[0, 0], precise)

    @pl.when(e == pl.num_programs(1) - 1)
    def _():
        o_ref[...] = h_ref[...] + acc_ref[...]


def moe_ffn(h, g_row, gates, wg, wu, wd, layer, *, tm):
    n, d = h.shape
    _, n_e, _, ffe = wg.shape
    precise = wg.dtype == F32
    return pl.pallas_call(
        functools.partial(_moe_kernel, precise=precise),
        grid=(n // tm, n_e),
        in_specs=[pl.BlockSpec((tm, d), lambda i, e: (i, 0)),
                  pl.BlockSpec((1, d), lambda i, e: (0, 0)),
                  pl.BlockSpec((tm, LANES), lambda i, e: (i, 0)),
                  pl.BlockSpec((1, 1, d, ffe), lambda i, e: (layer, e, 0, 0)),
                  pl.BlockSpec((1, 1, d, ffe), lambda i, e: (layer, e, 0, 0)),
                  pl.BlockSpec((1, 1, ffe, d), lambda i, e: (layer, e, 0, 0))],
        out_specs=pl.BlockSpec((tm, d), lambda i, e: (i, 0)),
        out_shape=jax.ShapeDtypeStruct((n, d), F32),
        scratch_shapes=[pltpu.VMEM((tm, d), F32 if precise else BF16), pltpu.VMEM((tm, d), F32)],
        compiler_params=_cparams("parallel", "arbitrary"),
        name="moe_ffn",
    )(h, g_row, gates, wg, wu, wd)


def _head_rows(width):
    r = lax.broadcasted_iota(jnp.int32, (8, width), 0)
    c = lax.broadcasted_iota(jnp.int32, (8, width), 1)
    return (c // HEAD_DIM) == r


def _lane_to_rows(row_vec):
    r = lax.broadcasted_iota(jnp.int32, (8, LANES), 0)
    c = lax.broadcasted_iota(jnp.int32, (8, LANES), 1)
    return jnp.sum(jnp.where(r == c, row_vec, 0.0), axis=1, keepdims=True)


def _finish_decode(acc, own, g_ref, o_ref):
    o = jnp.sum(jnp.where(own, acc, 0.0), axis=0, keepdims=True)
    o_ref[0] = _rms(o, g_ref[...])


def _sb_decode_kernel(pt_ref, q_ref, g_ref, *rest, pps):
    k_refs, v_refs = rest[:pps], rest[pps:2 * pps]
    o_ref, acc_ref, r_ref = rest[2 * pps:]
    c = pl.program_id(1)

    @pl.when(c == 0)
    def _():
        acc_ref[...] = jnp.zeros_like(acc_ref)
        r_ref[...] = jnp.zeros_like(r_ref)

    own = _head_rows(ATT_W)
    qb = jnp.where(own, q_ref[0] * (HEAD_DIM ** -0.5), 0.0)
    u = _tri(LANES, lambda r, cc: r >= cc)
    order = list(reversed(range(pps)))
    zs = [_dot3(qb, k_refs[i][0]) for i in order]
    css = [_dot_ones(_softplus(z), u) for z in zs]
    acc = acc_ref[...]
    run = r_ref[...]
    for i, z, cs in zip(order, zs, css):
        acc = acc + _dot3_nt(jnp.exp(z - cs - run), v_refs[i][0])
        run = run + cs[:, 0:1]
    acc_ref[...] = acc
    r_ref[...] = run

    @pl.when(c == pl.num_programs(1) - 1)
    def _():
        _finish_decode(acc, own, g_ref, o_ref)


def _page_specs(n_chunks, pps, page_off, block):
    def spec(i):
        def index(b, c, pt):
            return (page_off + pt[b, (n_chunks - 1 - c) * pps + i],) + (0,) * (len(block) - 1)
        return pl.BlockSpec(block, index)
    return [spec(i) for i in range(pps)]


def sb_decode(q3, g_row, cache_k, cache_v, page_table, *, page_off, pps):
    nb = q3.shape[0]
    n_pages = page_table.shape[1]
    n_chunks = n_pages // pps
    page = cache_k.shape[2]
    kern = functools.partial(_sb_decode_kernel, pps=pps)
    pages = _page_specs(n_chunks, pps, page_off, (1, ATT_W, page))
    grid_spec = pltpu.PrefetchScalarGridSpec(
        num_scalar_prefetch=1,
        grid=(nb, n_chunks),
        in_specs=[pl.BlockSpec((1, 1, ATT_W), lambda b, c, pt: (b, 0, 0)),
                  pl.BlockSpec((1, ATT_W), lambda b, c, pt: (0, 0))] + pages + pages,
        out_specs=pl.BlockSpec((1, 1, ATT_W), lambda b, c, pt: (b, 0, 0)),
        scratch_shapes=[pltpu.VMEM((8, ATT_W), F32), pltpu.VMEM((8, 1), F32)],
    )
    return pl.pallas_call(
        kern,
        grid_spec=grid_spec,
        out_shape=jax.ShapeDtypeStruct((nb, 1, ATT_W), F32),
        compiler_params=_cparams("parallel", "arbitrary"),
        name="sb_decode",
    )(page_table, q3, g_row, *([cache_k] * pps), *([cache_v] * pps))


def _fox_decode_kernel(pt_ref, q_ref, k_ref, v_ref, f_ref, gq_ref, gk_ref, fb_ref, g_ref, *rest, pps):
    k_refs, v_refs, f_refs = rest[:pps], rest[pps:2 * pps], rest[2 * pps:3 * pps]
    o_ref, kn_ref, lf_ref, acc_ref, m_ref, l_ref, r_ref, qb_ref = rest[3 * pps:]
    c = pl.program_id(1)
    own = _head_rows(ATT_W)

    @pl.when(c == 0)
    def _():
        def head_norm(x_row, gain_row):
            xb = jnp.where(own, x_row, 0.0)
            ms = jnp.sum(xb * xb, axis=1, keepdims=True) * (1.0 / HEAD_DIM)
            return xb * lax.rsqrt(ms + EPS) * gain_row

        qn = head_norm(q_ref[0], gq_ref[...]) * (HEAD_DIM ** -0.5)
        kn = head_norm(k_ref[0], gk_ref[...])
        kn_ref[0] = jnp.sum(kn, axis=0, keepdims=True)
        fx = f_ref[0] + fb_ref[...]
        lf = jnp.minimum(fx, 0.0) - jnp.log(1.0 + jnp.exp(-jnp.abs(fx)))
        lf_ref[0] = lf
        qb_ref[...] = qn
        m_ref[...] = jnp.sum(qn * kn, axis=1, keepdims=True)
        l_ref[...] = jnp.ones_like(l_ref)
        acc_ref[...] = jnp.broadcast_to(v_ref[0], acc_ref.shape)
        r_ref[...] = _lane_to_rows(lf)

    qb = qb_ref[...]
    u = _tri(LANES, lambda r, cc: r >= cc)
    m_old, run = m_ref[...], r_ref[...]
    pad = jnp.zeros((8 - N_HEADS, LANES), F32)
    order = list(reversed(range(pps)))
    zs = [_dot3(qb, k_refs[i][0]) for i in order]
    lfs =[jnp.concatenate([f_refs[i][0], pad], axis=0) for i in order]
    css = [_dot_ones(lf, u) for lf in lfs]
    ss = []
    for z, lf, cs in zip(zs, lfs, css):
        ss.append(z + (cs - lf) + run)
        run = run + cs[:, 0:1]
    m_new = m_old
    for s in ss:
        m_new = jnp.maximum(m_new, jnp.max(s, axis=1, keepdims=True))
    alpha = jnp.exp(m_old - m_new)
    l = alpha * l_ref[...]
    acc = alpha * acc_ref[...]
    for i, s in zip(order, ss):
        pr = jnp.exp(s - m_new)
        l = l + jnp.sum(pr, axis=1, keepdims=True)
        acc = acc + _dot3_nt(pr, v_refs[i][0])
    acc_ref[...], m_ref[...], l_ref[...], r_ref[...] = acc, m_new, l, run

    @pl.when(c == pl.num_programs(1) - 1)
    def _():
        _finish_decode(acc / l, own, g_ref, o_ref)


def fox_decode(q3, k3, v3, f3, gq_row, gk_row, fb_row, g_row, cache_k, cache_v, cache_ft, page_table,
               *, page_off, pps):
    nb = q3.shape[0]
    n_pages = page_table.shape[1]
    n_chunks = n_pages // pps
    page = cache_k.shape[2]
    kern = functools.partial(_fox_decode_kernel, pps=pps)
    pages = _page_specs(n_chunks, pps, page_off, (1, ATT_W, page))
    f_pages = _page_specs(n_chunks, pps, page_off, (1, N_HEADS, page))
    tok = lambda w: pl.BlockSpec((1, 1, w), lambda b, c, pt: (b, 0, 0))
    par = lambda w: pl.BlockSpec((1, w), lambda b, c, pt: (0, 0))
    grid_spec = pltpu.PrefetchScalarGridSpec(
        num_scalar_prefetch=1,
        grid=(nb, n_chunks),
        in_specs=[tok(ATT_W), tok(ATT_W), tok(ATT_W), tok(LANES),
                  par(ATT_W), par(ATT_W), par(LANES), par(ATT_W)] + pages + pages + f_pages,
        out_specs=[tok(ATT_W), tok(ATT_W), tok(LANES)],
        scratch_shapes=[pltpu.VMEM((8, ATT_W), F32), pltpu.VMEM((8, 1), F32), pltpu.VMEM((8, 1), F32),
                        pltpu.VMEM((8, 1), F32), pltpu.VMEM((8, ATT_W), F32)],
    )
    return pl.pallas_call(
        kern,
        grid_spec=grid_spec,
        out_shape=[jax.ShapeDtypeStruct((nb, 1, ATT_W), F32),
                   jax.ShapeDtypeStruct((nb, 1, ATT_W), F32),
                   jax.ShapeDtypeStruct((nb, 1, LANES), F32)],
        compiler_params=_cparams("parallel", "arbitrary"),
        name="fox_decode",
    )(page_table, q3, k3, v3, f3, gq_row, gk_row, fb_row, g_row,
      *([cache_k] * pps), *([cache_v] * pps), *([cache_ft] * pps))


def _row_to_col(row_vec):
    r = lax.broadcasted_iota(jnp.int32, (HG_D, HG_D), 0)
    c = lax.broadcasted_iota(jnp.int32, (HG_D, HG_D), 1)
    return jnp.sum(jnp.where(r == c, row_vec, 0.0), axis=1, keepdims=True)


def _hgrn_decode_kernel(hq_ref, hf_ref, hi_ref, hg_ref, lb_ref, g_ref, s_ref, o_ref, sn_ref):
    outs = []
    for h in range(N_HEADS):
        ws = slice(h * HG_D, (h + 1) * HG_D)
        q, f, k = _hgrn_gates(hq_ref[0][:, ws], hf_ref[0][:, ws], lb_ref[:, ws])
        s_new = s_ref[0, h] * _row_to_col(f) + _row_to_col(k) * hi_ref[0][:, ws]
        sn_ref[0, h] = s_new
        o = jnp.sum(_row_to_col(q) * s_new, axis=0, keepdims=True)
        outs.append(_rms(o, g_ref[...]) * _sigmoid(hg_ref[0][:, ws]))
    o_ref[0] = jnp.concatenate(outs, axis=1)


def hgrn_decode(hq3, hf3, hi3, hg3, lb_row, g_row, state):
    nb = hq3.shape[0]
    tok = pl.BlockSpec((1, 1, HG_W), lambda b: (b, 0, 0))
    st = pl.BlockSpec((1, N_HEADS, HG_D, HG_D), lambda b: (b, 0, 0, 0))
    return pl.pallas_call(
        _hgrn_decode_kernel,
        grid=(nb,),
        in_specs=[tok, tok, tok, tok,
                  pl.BlockSpec((1, HG_W), lambda b: (0, 0)),
                  pl.BlockSpec((1, HG_D), lambda b: (0, 0)), st],
        out_specs=[tok, st],
        out_shape=[jax.ShapeDtypeStruct((nb, 1, HG_W), F32),
                   jax.ShapeDtypeStruct(state.shape, F32)],
        compiler_params=_cparams("parallel"),
        name="hgrn_decode",
    )(hq3, hf3, hi3, hg3, lb_row, g_row, state)


def _arrange_w_in(w_in_l):
    d = w_in_l.shape[0]
    a = 2 * 3 * ATT_W
    main = jnp.concatenate([w_in_l[:, :a], w_in_l[:, a + N_HEADS:]], axis=1)
    fcols = w_in_l[:, a:a + N_HEADS]
    pad = jnp.zeros((d, N_IN_PAD - C_FXF - N_HEADS), w_in_l.dtype)
    return jnp.concatenate([main, fcols, pad], axis=1).astype(F32)


def _seg_matrix():
    r = lax.broadcasted_iota(jnp.int32, (ATT_W, ATT_W), 0) // HEAD_DIM
    c = lax.broadcasted_iota(jnp.int32, (ATT_W, ATT_W), 1) // HEAD_DIM
    return jnp.where(r == c, 1.0, 0.0).astype(BF16)


def _row(v, width=None):
    v = v.reshape(1, -1).astype(F32)
    if width is not None and v.shape[1] < width:
        v = jnp.pad(v, ((0, 0), (0, width - v.shape[1])))
    return v


def kernel(x_prompt, x_sample, cache_sb_k, cache_sb_v, cache_fox_k, cache_fox_v, cache_fox_logf, state_hgrn,
           page_table, w_in, w_out, norm_mix_g, norm_ffn_g, fox_q_norm_g, fox_k_norm_g, fox_f_bias, sb_out_g,
           fox_out_g, hgrn_out_g, hgrn_lb_logits, ffn_w_gate, ffn_w_up, ffn_w_down, moe_router_w,
           moe_router_b, moe_w_gate, moe_w_up, moe_w_down):
    batch, seq, d_model = x_prompt.shape
    dec_b = x_sample.shape[0]
    depth = w_in.shape[0]
    n_pool, page = cache_sb_k.shape[1], cache_sb_k.shape[2]
    n_experts = moe_router_w.shape[-1]

    lb_all = lower_bounds(hgrn_lb_logits.astype(F32))
    seg = _seg_matrix()
    def pages_t(c):
        return jnp.transpose(c, (0, 1, 3, 4, 2)).reshape(depth * n_pool, ATT_W, page)

    sbk_c, sbv_c, fxk_c, fxv_c = (pages_t(c) for c in (cache_sb_k, cache_sb_v, cache_fox_k, cache_fox_v))
    fxf_c = jnp.swapaxes(cache_fox_logf, 2, 3).reshape(depth * n_pool, N_HEADS, page)

    xp = x_prompt.reshape(batch * seq, d_model)
    xs = x_sample.reshape(dec_b, d_model)
    p_out = [[] for _ in range(6)]
    s_out = [[] for _ in range(6)]
    kv_stacks, fox_stacks = None, None

    for l in range(depth):
        w_in_f = _arrange_w_in(w_in[l])
        w_in_l = w_in_f.astype(BF16)
        w_out_f = w_out[l].astype(F32)
        w_out_l = w_out_f.astype(BF16)
        g_mix, g_ffn = _row(norm_mix_g[l]), _row(norm_ffn_g[l])
        gq = _row(jnp.tile(fox_q_norm_g[l], N_HEADS))
        gk = _row(jnp.tile(fox_k_norm_g[l], N_HEADS))
        fb = _row(fox_f_bias[l], LANES)
        g_sb, g_fx, g_hg = _row(sb_out_g[l]), _row(fox_out_g[l]), _row(hgrn_out_g[l])
        lb_row = lb_all[l:l + 1]

        p, kv_stacks = in_proj_stacked(xp, g_mix, w_in_l, kv_stacks, l, depth, batch, tm=512, tn=1280)
        kn_stack, lf_stack, qa, ka = fox_prep(p, gq, gk, fb, seg, fox_stacks, l, depth,
                                              batch=batch, seq=seq, tb=256)
        fox_stacks = (kn_stack, lf_stack)
        sb_o = sb_attention(p, g_sb, batch=batch, seq=seq, tq=512, tk=256)
        fx_o = fox_attention(qa, ka, p, g_fx, batch=batch, seq=seq, tq=256)
        hg_o, s_fin = hgrn_prompt(p, lb_row, g_hg, batch=batch, seq=seq, chunk=128, sub=8)
        hp = out_proj(xp, sb_o, fx_o, hg_o, w_out_l, tm=512)
        p_out[5].append(s_fin)

        ps = in_proj(xs, g_mix, w_in_f, tm=dec_b, tn=1280)
        tok = lambda c0, w: ps[:, c0:c0 + w].reshape(dec_b, 1, w)
        off = l * n_pool
        sb_os = sb_decode(tok(C_SBQ, ATT_W), g_sb, sbk_c, sbv_c, page_table, page_off=off, pps=32)
        fx_os, fxk_s, lf_s = fox_decode(tok(C_FXQ, ATT_W), tok(C_FXK, ATT_W), tok(C_FXV, ATT_W),
                                        tok(C_FXF, LANES), gq, gk, fb, g_fx, fxk_c, fxv_c, fxf_c,
                                        page_table, page_off=off, pps=32)
        hg_os, s_new = hgrn_decode(tok(C_HGQ, HG_W), tok(C_HGF, HG_W), tok(C_HGI, HG_W), tok(C_HGG, HG_W),
                                   lb_row, g_hg, state_hgrn[l].astype(F32))
        hs = out_proj(xs, sb_os.reshape(dec_b, ATT_W), fx_os.reshape(dec_b, ATT_W),
                      hg_os.reshape(dec_b, HG_W), w_out_f, tm=dec_b)
        s_out[0].append(ps[:, C_SBK:C_SBK + ATT_W])
        s_out[1].append(ps[:, C_SBV:C_SBV + ATT_W])
        s_out[2].append(fxk_s.reshape(dec_b, ATT_W))
        s_out[3].append(ps[:, C_FXV:C_FXV + ATT_W])
        s_out[4].append(lf_s.reshape(dec_b, LANES)[:, :N_HEADS])
        s_out[5].append(s_new)

        if l % 2 == 0:
            wf = [w.astype(F32) for w in (ffn_w_gate, ffn_w_up, ffn_w_down)]
            wg, wu, wd = (w[l // 2][None].astype(BF16) for w in wf)
            xp = ffn_dense(hp, g_ffn, wg, wu, wd, 0, tm=1024, tf=256)
            xs = ffn_dense(hs, g_ffn, *wf, l // 2, tm=dec_b, tf=256)
        else:
            wf = [w.astype(F32) for w in (moe_w_gate, moe_w_up, moe_w_down)]
            wg, wu, wd = (w[l // 2][None].astype(BF16) for w in wf)
            rw = jnp.pad(moe_router_w[l // 2].astype(F32), ((0, 0), (0, LANES - n_experts)))
            rb = _row(moe_router_b[l // 2], LANES)
            gates_p = moe_router(hp, g_ffn, rw, rb, tm=512, n_experts=n_experts)
            xp = moe_ffn(hp, g_ffn, gates_p, wg, wu, wd, 0, tm=512)
            gates_s = moe_router(hs, g_ffn, rw, rb, tm=dec_b, n_experts=n_experts)
            xs = moe_ffn(hs, g_ffn, gates_s, *wf, l // 2, tm=dec_b)

    def stack(parts, shape):
        return jnp.stack(parts).reshape((depth,) + shape)

    hd = (N_HEADS, HEAD_DIM)
    y_prompt = xp.reshape(batch, seq, d_model)
    y_sample = xs.reshape(dec_b, 1, d_model)
    sbk_p, sbv_p, fxv_p = kv_stacks
    kn_stack, lf_stack = fox_stacks
    outs_p = [jnp.transpose(a.reshape((depth, batch) + hd + (seq,)), (0, 1, 4, 2, 3))
              for a in (sbk_p, sbv_p, kn_stack, fxv_p)]
    outs_p.append(jnp.swapaxes(lf_stack, 2, 3))
    outs_p.append(stack(p_out[5], (batch, N_HEADS, HG_D, HG_D)).astype(state_hgrn.dtype))
    outs_s = [stack(s_out[i], (dec_b, 1) + hd) for i in range(4)]
    outs_s.append(stack(s_out[4], (dec_b, 1, N_HEADS)))
    outs_s.append(stack(s_out[5], (dec_b, N_HEADS, HG_D, HG_D)).astype(state_hgrn.dtype))
    return (y_prompt, y_sample, *outs_p, *outs_s)
```

```python
import functools

import jax
import jax.numpy as jnp
from jax import lax
from jax.experimental import pallas as pl
from jax.experimental.pallas import tpu as pltpu

F32 = jnp.float32
BF16 = jnp.bfloat16
EPS = 1e-6
NEG_BIG = -1e30

HEAD_DIM = 64
N_HEADS = 4
ATT_W = N_HEADS * HEAD_DIM
HG_D = 128
HG_W = N_HEADS * HG_D
LANES = 128
VMEM_LIMIT = 56 * 1024 * 1024

C_SBQ, C_SBK, C_SBV = 0, 256, 512
C_FXQ, C_FXK, C_FXV = 768, 1024, 1280
C_HGQ, C_HGF, C_HGI, C_HGG = 1536, 2048, 2560, 3072
C_FXF = 3584
N_IN_PAD = 3840


def _cparams(*sem):
    return pltpu.CompilerParams(dimension_semantics=sem, vmem_limit_bytes=VMEM_LIMIT)


def _rms(xf, g_row):
    ms = jnp.mean(xf * xf, axis=-1, keepdims=True)
    return xf * lax.rsqrt(ms + EPS) * g_row


def _softplus(z):
    return jnp.maximum(z, 0.0) + jnp.log(1.0 + jnp.exp(-jnp.abs(z)))


def _sigmoid(x):
    return 1.0 / (1.0 + jnp.exp(-x))


def _split2(x):
    hi = x.astype(BF16)
    lo = (x - hi.astype(F32)).astype(BF16)
    return hi, lo


def _split3(x):
    hi = x.astype(BF16)
    r = x - hi.astype(F32)
    mid = r.astype(BF16)
    lo = (r - mid.astype(F32)).astype(BF16)
    return hi, mid, lo


def _dot(a, b):
    return jnp.dot(a, b, preferred_element_type=F32)


def _dot_nt(a, b):
    return lax.dot_general(a, b, (((1,), (1,)), ((), ())), preferred_element_type=F32)


def _stack_split2(x):
    hi, lo = _split2(x)
    return hi, jnp.concatenate([hi, lo], axis=0)


def _dot3(x, w):
    m = x.shape[0]
    xh, x2 = _stack_split2(x)
    wh, wl = _split2(w)
    r = _dot(x2, wh)
    return r[:m] + r[m:] + _dot(xh, wl)


def _dot3_nt(x, w):
    m = x.shape[0]
    xh, x2 = _stack_split2(x)
    wh, wl = _split2(w)
    r = _dot_nt(x2, wh)
    return r[:m] + r[m:] + _dot_nt(xh, wl)


def _dot_ones(x, ones_bf16):
    m = x.shape[0]
    r = _dot(jnp.concatenate(_split3(x), axis=0), ones_bf16)
    return r[:m] + r[m:2 * m] + r[2 * m:]


def _mm(x, w, precise):
    return _dot3(x, w) if precise else _dot(x.astype(BF16), w)


def _tri(n, rel):
    r = lax.broadcasted_iota(jnp.int32, (n, n), 0)
    c = lax.broadcasted_iota(jnp.int32, (n, n), 1)
    return jnp.where(rel(r, c), 1.0, 0.0).astype(BF16)


def _lower_bounds_kernel(x_ref, o_ref):
    x = x_ref[...]
    depth = x.shape[0]
    m = jnp.max(x, axis=0, keepdims=True)
    e = jnp.exp(x - m)
    sm = e / jnp.sum(e, axis=0, keepdims=True)
    run = jnp.zeros_like(sm[0:1])
    rows = []
    for l in range(depth):
        run = run + sm[l:l + 1]
        rows.append(run - sm[0:1])
    o_ref[...] = jnp.concatenate(rows, axis=0)


def lower_bounds(logits):
    return pl.pallas_call(
        _lower_bounds_kernel,
        out_shape=jax.ShapeDtypeStruct(logits.shape, F32),
        name="hgrn_lower_bounds",
    )(logits)


def _in_proj_kernel(x_ref, g_ref, w_ref, o_ref, xn_ref, *, precise):
    @pl.when(pl.program_id(1) == 0)
    def _():
        xn_ref[...] = _rms(x_ref[...], g_ref[...]).astype(xn_ref.dtype)

    o_ref[...] = _mm(xn_ref[...], w_ref[...], precise)


def in_proj(x, g_row, w, *, tm, tn):
    n, d = x.shape
    n_out = w.shape[1]
    precise = w.dtype == F32
    return pl.pallas_call(
        functools.partial(_in_proj_kernel, precise=precise),
        grid=(n // tm, n_out // tn),
        in_specs=[pl.BlockSpec((tm, d), lambda i, j: (i, 0)),
                  pl.BlockSpec((1, d), lambda i, j: (0, 0)),
                  pl.BlockSpec((d, tn), lambda i, j: (0, j))],
        out_specs=pl.BlockSpec((tm, tn), lambda i, j: (i, j)),
        out_shape=jax.ShapeDtypeStruct((n, n_out), F32),
        scratch_shapes=[pltpu.VMEM((tm, d), F32 if precise else BF16)],
        compiler_params=_cparams("parallel", "arbitrary"),
        name="norm_in_proj",
    )(x, g_row, w)


STACKED_COLS = (C_SBK, C_SBV, C_FXV)


def _in_proj_stack_kernel(x_ref, g_ref, w_ref, *rest, tn, n_alias):
    o_ref, s_refs, xn_ref = rest[n_alias], rest[n_alias + 1:-1], rest[-1]
    j = pl.program_id(1)

    @pl.when(j == 0)
    def _():
        xn_ref[...] = _rms(x_ref[...], g_ref[...]).astype(BF16)

    res = _dot(xn_ref[...], w_ref[...])
    o_ref[...] = res
    for c0, s_ref in zip(STACKED_COLS, s_refs):
        @pl.when(j == c0 // tn)
        def _(c0=c0, s_ref=s_ref):
            s_ref[0, 0] = res[:, c0 % tn:c0 % tn + ATT_W].T


def in_proj_stacked(x, g_row, w_bf16, stacks, layer, depth, batch, *, tm, tn):
    n, d = x.shape
    n_out = w_bf16.shape[1]
    seq = n // batch
    tiles = seq // tm
    assert all(c0 % tn + ATT_W <= tn for c0 in STACKED_COLS) and seq % tm == 0
    n_alias = 0 if stacks is None else len(STACKED_COLS)
    stack_sds = jax.ShapeDtypeStruct((depth, batch, ATT_W, seq), F32)
    outs = pl.pallas_call(
        functools.partial(_in_proj_stack_kernel, tn=tn, n_alias=n_alias),
        grid=(n // tm, n_out // tn),
        in_specs=[pl.BlockSpec((tm, d), lambda i, j: (i, 0)),
                  pl.BlockSpec((1, d), lambda i, j: (0, 0)),
                  pl.BlockSpec((d, tn), lambda i, j: (0, j))]
                 + [pl.BlockSpec(memory_space=pl.ANY)] * n_alias,
        out_specs=[pl.BlockSpec((tm, tn), lambda i, j: (i, j))]
                  + [pl.BlockSpec((1, 1, ATT_W, tm), lambda i, j: (layer, i // tiles, 0, i % tiles))]
                  * len(STACKED_COLS),
        out_shape=[jax.ShapeDtypeStruct((n, n_out), F32)] + [stack_sds] * len(STACKED_COLS),
        input_output_aliases={3 + a: 1 + a for a in range(n_alias)},
        scratch_shapes=[pltpu.VMEM((tm, d), BF16)],
        compiler_params=_cparams("parallel", "arbitrary"),
        name="norm_in_proj_stacked",
    )(x, g_row, w_bf16, *(stacks or ()))
    return outs[0], tuple(outs[1:])


def _seg_mean_sq(x, seg_ref):
    hi, lo = _split2(x * x)
    return (_dot(hi, seg_ref[...]) + _dot(lo, seg_ref[...])) * (1.0 / HEAD_DIM)


def _fox_prep_kernel(q_ref, k_ref, f_ref, gq_ref, gk_ref, fb_ref, seg_ref, *rest):
    kn_ref, lf_ref, qa_ref, ka_ref, carry_ref = rest[-5:]
    tb = q_ref.shape[0]

    @pl.when(pl.program_id(1) == 0)
    def _():
        carry_ref[...] = jnp.zeros_like(carry_ref)

    xq = q_ref[...]
    xk = k_ref[...]
    qn = xq * lax.rsqrt(_seg_mean_sq(xq, seg_ref) + EPS) * gq_ref[...]
    kn = xk * lax.rsqrt(_seg_mean_sq(xk, seg_ref) + EPS) * gk_ref[...]
    kn_ref[0, 0] = kn.T

    fx = f_ref[...] + fb_ref[...]
    lf = jnp.minimum(fx, 0.0) - jnp.log(1.0 + jnp.exp(-jnp.abs(fx)))
    lf_ref[0, 0] = lf.T[:N_HEADS]

    ltri = _tri(tb, lambda r, c: c <= r)
    p0, p1, p2 = _split3(lf)
    cum = _dot(ltri, p0) + _dot(ltri, p1) + _dot(ltri, p2) + carry_ref[...]
    carry_ref[...] = cum[tb - 1:tb, :]

    c0, c1, c2 = (c.astype(F32) for c in _split3(cum))
    lane = lax.broadcasted_iota(jnp.int32, (tb, HEAD_DIM), 1)
    one = jnp.where(lane < 6, 1.0, 0.0)
    qs = qn * (HEAD_DIM ** -0.5)
    q_parts, k_parts = [], []
    for h in range(N_HEADS):
        a0, a1, a2 = c0[:, h:h + 1], c1[:, h:h + 1], c2[:, h:h + 1]
        ext_q = jnp.where(lane == 0, a0, jnp.where(lane == 1, a1, jnp.where(lane == 2, a2, one)))
        ext_k = jnp.where(lane == 3, -a0, jnp.where(lane == 4, -a1, jnp.where(lane == 5, -a2, one)))
        q_parts += [qs[:, h * HEAD_DIM:(h + 1) * HEAD_DIM], ext_q]
        k_parts += [kn[:, h * HEAD_DIM:(h + 1) * HEAD_DIM], ext_k]
    qa_ref[...] = jnp.concatenate(q_parts, axis=1).astype(BF16)
    ka_ref[...] = jnp.concatenate(k_parts, axis=1).astype(BF16)


def fox_prep(p, gq_row, gk_row, fb_row, seg, stacks, layer, depth, *, batch, seq, tb):
    n = p.shape[0]
    nb = seq // tb
    row = lambda b, t: b * nb + t
    aw = N_HEADS * LANES
    alias = () if stacks is None else tuple(stacks)
    return pl.pallas_call(
        _fox_prep_kernel,
        grid=(batch, nb),
        in_specs=[pl.BlockSpec((tb, ATT_W), lambda b, t: (row(b, t), C_FXQ // ATT_W)),
                  pl.BlockSpec((tb, ATT_W), lambda b, t: (row(b, t), C_FXK // ATT_W)),
                  pl.BlockSpec((tb, LANES), lambda b, t: (row(b, t), C_FXF // LANES)),
                  pl.BlockSpec((1, ATT_W), lambda b, t: (0, 0)),
                  pl.BlockSpec((1, ATT_W), lambda b, t: (0, 0)),
                  pl.BlockSpec((1, LANES), lambda b, t: (0, 0)),
                  pl.BlockSpec((ATT_W, ATT_W), lambda b, t: (0, 0))]
                 + [pl.BlockSpec(memory_space=pl.ANY)] * len(alias),
        out_specs=[pl.BlockSpec((1, 1, ATT_W, tb), lambda b, t: (layer, b, 0, t)),
                   pl.BlockSpec((1, 1, N_HEADS, tb), lambda b, t: (layer, b, 0, t)),
                   pl.BlockSpec((tb, aw), lambda b, t: (row(b, t), 0)),
                   pl.BlockSpec((tb, aw), lambda b, t: (row(b, t), 0))],
        out_shape=[jax.ShapeDtypeStruct((depth, batch, ATT_W, seq), F32),
                   jax.ShapeDtypeStruct((depth, batch, N_HEADS, seq), F32),
                   jax.ShapeDtypeStruct((n, aw), BF16),
                   jax.ShapeDtypeStruct((n, aw), BF16)],
        input_output_aliases={7: 0, 8: 1} if alias else {},
        scratch_shapes=[pltpu.VMEM((1, LANES), F32)],
        compiler_params=_cparams("parallel", "arbitrary"),
        name="fox_prep",
    )(p, p, p, gq_row, gk_row, fb_row, seg, *alias)


def _store_v_transposed(v_ref, vt_ref, tk):
    for j in range(vt_ref.shape[0]):
        vt_ref[j] = v_ref[j * tk:(j + 1) * tk, :].T.astype(BF16)


def _finish_attention(ot, g_ref, o_ref):
    ms = jnp.mean(ot * ot, axis=0, keepdims=True)
    o_ref[...] = (ot * lax.rsqrt(ms + EPS)).T * g_ref[...]


def _sb_attn_kernel(q_ref, k_ref, v_ref, g_ref, o_ref, acc_ref, r_ref, *, tk):
    i = pl.program_id(1)
    tq = q_ref.shape[0]
    acc_ref[...] = jnp.zeros_like(acc_ref)
    r_ref[...] = jnp.zeros_like(r_ref)
    u = _tri(tk, lambda r, c: r >= c)
    n_diag = tq // tk
    qry_i = lax.broadcasted_iota(jnp.int32, (tq, tk), 0)
    key_i = lax.broadcasted_iota(jnp.int32, (tq, tk), 1)
    low_half = lax.broadcasted_iota(jnp.int32, (tq, LANES), 1) < HEAD_DIM
    qm = []
    for pair in range(N_HEADS // 2):
        qp = q_ref[:, pair * LANES:(pair + 1) * LANES] * (HEAD_DIM ** -0.5)
        qm.append(jnp.where(low_half, qp, 0.0).astype(BF16))
        qm.append(jnp.where(low_half, 0.0, qp).astype(BF16))

    def block(j, diag_off):
        row0 = pl.multiple_of(j * tk, tk)
        before = None if diag_off is None else (key_i + diag_off < qry_i)
        for pair in range(N_HEADS // 2):
            ls = slice(pair * LANES, (pair + 1) * LANES)
            kp = k_ref[pl.ds(row0, tk), ls].astype(BF16)
            vp = v_ref[pl.ds(row0, tk), ls].astype(BF16)
            res = []
            for h in (2 * pair, 2 * pair + 1):
                z = _dot_nt(qm[h], kp)
                sp = _softplus(z)
                if before is not None:
                    sp = jnp.where(before, sp, 0.0)
                cs = _dot(sp.astype(BF16), u)
                r_old = r_ref[h]
                a = jnp.exp(z - cs - r_old)
                if before is not None:
                    a = jnp.where(before, a, 0.0)
                res.append(_dot(a.astype(BF16), vp))
                r_ref[h] = r_old + cs[:, 0:1]
            acc_ref[:, ls] += jnp.where(low_half, res[0], res[1])

    for d in reversed(range(n_diag)):
        block(i * n_diag + d, d * tk)

    assert n_diag % 2 == 0

    def body(jj, c):
        block(i * n_diag - 1 - 2 * jj, None)
        block(i * n_diag - 2 - 2 * jj, None)
        return c

    lax.fori_loop(0, i * (n_diag // 2), body, 0)
    o_ref[...] = _rms(acc_ref[...], g_ref[...])


def sb_attention(p, g_row, *, batch, seq, tq, tk):
    n = p.shape[0]
    nq = seq // tq
    assert tq % tk == 0
    kern = functools.partial(_sb_attn_kernel, tk=tk)
    return pl.pallas_call(
        kern,
        grid=(batch, nq),
        in_specs=[pl.BlockSpec((tq, ATT_W), lambda b, i: (b * nq + i, C_SBQ // ATT_W)),
                  pl.BlockSpec((seq, ATT_W), lambda b, i: (b, C_SBK // ATT_W)),
                  pl.BlockSpec((seq, ATT_W), lambda b, i: (b, C_SBV // ATT_W)),
                  pl.BlockSpec((1, ATT_W), lambda b, i: (0, 0))],
        out_specs=pl.BlockSpec((tq, ATT_W), lambda b, i: (b * nq + i, 0)),
        out_shape=jax.ShapeDtypeStruct((n, ATT_W), F32),
        scratch_shapes=[pltpu.VMEM((tq, ATT_W), F32),
                        pltpu.VMEM((N_HEADS, tq, 1), F32)],
        compiler_params=_cparams("parallel", "arbitrary"),
        name="sb_attention",
    )(p, p, p, g_row)


def _fox_attn_kernel(q_ref, k_ref, v_ref, g_ref, o_ref, vt_ref, acc_ref, m_ref, l_ref, *, tk):
    i = pl.program_id(1)
    tq = q_ref.shape[0]

    @pl.when(i == 0)
    def _():
        _store_v_transposed(v_ref, vt_ref, tk)

    acc_ref[...] = jnp.zeros_like(acc_ref)
    m_ref[...] = jnp.full_like(m_ref, NEG_BIG)
    l_ref[...] = jnp.zeros_like(l_ref)
    key_i = lax.broadcasted_iota(jnp.int32, (tk, tq), 0)
    qry_i = lax.broadcasted_iota(jnp.int32, (tk, tq), 1)
    causal = key_i <= qry_i

    def blocks(js, diagonal):
        sts = [[_dot_nt(k_ref[pl.ds(pl.multiple_of(j * tk, tk), tk), h * LANES:(h + 1) * LANES],
                        q_ref[:, h * LANES:(h + 1) * LANES]) for j in js] for h in range(N_HEADS)]
        for h in range(N_HEADS):
            hs = slice(h * HEAD_DIM, (h + 1) * HEAD_DIM)
            st_h = sts[h]
            if diagonal:
                st_h = [jnp.where(causal, st, NEG_BIG) for st in st_h]
            m_old = m_ref[h:h + 1, :]
            m_new = m_old
            for st in st_h:
                m_new = jnp.maximum(m_new, jnp.max(st, axis=0, keepdims=True))
            alpha = jnp.exp(m_old - m_new)
            l_new = alpha * l_ref[h:h + 1, :]
            acc_new = alpha * acc_ref[hs, :]
            for j, st in zip(js, st_h):
                pt = jnp.exp(st - m_new)
                l_new = l_new + jnp.sum(pt, axis=0, keepdims=True)
                acc_new = acc_new + _dot(vt_ref[j, hs, :], pt.astype(BF16))
            l_ref[h:h + 1, :] = l_new
            acc_ref[hs, :] = acc_new
            m_ref[h:h + 1, :] = m_new

    blocks([i], True)

    def body(jj, c):
        blocks([i - 1 - 4 * jj - d for d in range(4)], False)
        return c

    lax.fori_loop(0, i // 4, body, 0)
    rem = i % 4

    @pl.when(rem >= 2)
    def _():
        blocks([rem - 1, rem - 2], False)

    @pl.when(rem % 2 == 1)
    def _():
        blocks([0], False)
    parts = [acc_ref[h * HEAD_DIM:(h + 1) * HEAD_DIM, :] / l_ref[h:h + 1, :] for h in range(N_HEADS)]
    _finish_attention(jnp.concatenate(parts, axis=0), g_ref, o_ref)


def fox_attention(qa, ka, p, g_row, *, batch, seq, tq):
    n = p.shape[0]
    nq = seq // tq
    aw = N_HEADS * LANES
    kern = functools.partial(_fox_attn_kernel, tk=tq)
    return pl.pallas_call(
        kern,
        grid=(batch, nq),
        in_specs=[pl.BlockSpec((tq, aw), lambda b, i: (b * nq + i, 0)),
                  pl.BlockSpec((seq, aw), lambda b, i: (b, 0)),
                  pl.BlockSpec((seq, ATT_W), lambda b, i: (b, C_FXV // ATT_W)),
                  pl.BlockSpec((1, ATT_W), lambda b, i: (0, 0))],
        out_specs=pl.BlockSpec((tq, ATT_W), lambda b, i: (b * nq + i, 0)),
        out_shape=jax.ShapeDtypeStruct((n, ATT_W), F32),
        scratch_shapes=[pltpu.VMEM((nq, ATT_W, tq), BF16),
                        pltpu.VMEM((ATT_W, tq), F32),
                        pltpu.VMEM((8, tq), F32),
                        pltpu.VMEM((8, tq), F32)],
        compiler_params=_cparams("parallel", "arbitrary"),
        name="fox_attention",
    )(qa, ka, p, g_row)


def _hgrn_gates(hq, hf, lb):
    q = hq * _sigmoid(hq)
    f = lb + (1.0 - lb) * _sigmoid(hf)
    return q, f, 1.0 - f


def _hgrn_kernel(hq_ref, hf_ref, hi_ref, hg_ref, lb_ref, g_ref, o_ref, s_ref, st_ref, *, sub):
    c = hq_ref.shape[0]
    n_sub = c // sub

    @pl.when(pl.program_id(1) == 0)
    def _():
        st_ref[...] = jnp.zeros_like(st_ref)

    ltri = _tri(c, lambda r, cc: cc <= r)
    t_i = lax.broadcasted_iota(jnp.int32, (c, c), 0)
    s_i = lax.broadcasted_iota(jnp.int32, (c, c), 1)
    pos = lax.broadcasted_iota(jnp.int32, (c, HG_D), 0)
    pos_sub = lax.broadcasted_iota(jnp.int32, (n_sub, sub, 1), 1)

    q_all, f_all, k_all = _hgrn_gates(hq_ref[...], hf_ref[...], lb_ref[...])
    p0, p1, p2 = _split3(jnp.log(f_all))
    g3 = _dot(ltri, jnp.concatenate([p0, p1, p2], axis=1))
    g_all = g3[:, :HG_W] + g3[:, HG_W:2 * HG_W] + g3[:, 2 * HG_W:]
    gate_all = _sigmoid(hg_ref[...])

    outs = []
    for h in range(N_HEADS):
        ws = slice(h * HG_D, (h + 1) * HG_D)
        q, k, g, v = q_all[:, ws], k_all[:, ws], g_all[:, ws], hi_ref[:, ws]
        st = st_ref[h]
        o = _dot_nt((q * jnp.exp(g)).astype(BF16), st.astype(BF16))

        q3, k3, g3d, v3 = (a.reshape(n_sub, sub, HG_D) for a in (q, k, g, v))
        o_diag = jnp.zeros((n_sub, sub, HG_D), F32)
        for s in range(sub):
            dec = jnp.exp(jnp.minimum(g3d - g3d[:, s:s + 1, :], 0.0))
            col = jnp.sum(q3 * k3[:, s:s + 1, :] * dec, axis=-1, keepdims=True)
            o_diag = o_diag + jnp.where(pos_sub >= s, col, 0.0) * v3[:, s:s + 1, :]
        o = o + o_diag.reshape(c, HG_D)

        a = jnp.zeros((c, c), F32)
        blk = 2 * sub
        while blk <= c:
            half = blk // 2
            gr = g.reshape(c // blk, blk, HG_D)[:, half - 1:half, :]
            e = jnp.exp(-jnp.abs(g.reshape(c // blk, blk, HG_D) - gr)).reshape(c, HG_D)
            late = (pos & (blk - 1)) >= half
            qs = jnp.where(late, q * e, 0.0).astype(BF16)
            ks = jnp.where(late, 0.0, k * e).astype(BF16)
            a_l = _dot_nt(qs, ks)
            sh = blk.bit_length() - 1
            a = a + (a_l if blk == c else jnp.where((t_i >> sh) == (s_i >> sh), a_l, 0.0))
            blk *= 2
        o = o + _dot(a.astype(BF16), v.astype(BF16))

        g_end = g[c - 1:c, :]
        k_end = (k * jnp.exp(g_end - g)).astype(BF16)
        st_ref[h] = st * jnp.exp(g_end) + _dot(v.T.astype(BF16), k_end)
        outs.append(_rms(o, g_ref[...]) * gate_all[:, ws])
    o_ref[...] = jnp.concatenate(outs, axis=1)

    @pl.when(pl.program_id(1) == pl.num_programs(1) - 1)
    def _():
        for h in range(N_HEADS):
            s_ref[0, h] = st_ref[h].T


def hgrn_prompt(p, lb_row, g_row, *, batch, seq, chunk, sub):
    n = p.shape[0]
    nc = seq // chunk
    col = lambda c0: (lambda b, t: (b * nc + t, c0 // HG_W))
    kern = functools.partial(_hgrn_kernel, sub=sub)
    return pl.pallas_call(
        kern,
        grid=(batch, nc),
        in_specs=[pl.BlockSpec((chunk, HG_W), col(C_HGQ)),
                  pl.BlockSpec((chunk, HG_W), col(C_HGF)),
                  pl.BlockSpec((chunk, HG_W), col(C_HGI)),
                  pl.BlockSpec((chunk, HG_W), col(C_HGG)),
                  pl.BlockSpec((1, HG_W), lambda b, t: (0, 0)),
                  pl.BlockSpec((1, HG_D), lambda b, t: (0, 0))],
        out_specs=[pl.BlockSpec((chunk, HG_W), lambda b, t: (b * nc + t, 0)),
                   pl.BlockSpec((1, N_HEADS, HG_D, HG_D), lambda b, t: (b, 0, 0, 0))],
        out_shape=[jax.ShapeDtypeStruct((n, HG_W), F32),
                   jax.ShapeDtypeStruct((batch, N_HEADS, HG_D, HG_D), F32)],
        scratch_shapes=[pltpu.VMEM((N_HEADS, HG_D, HG_D), F32)],
        compiler_params=_cparams("parallel", "arbitrary"),
        name="hgrn_prompt",
    )(p, p, p, p, lb_row, g_row)


def _out_proj_kernel(x_ref, a_ref, b_ref, c_ref, w_ref, o_ref, *, precise):
    mix = jnp.concatenate([a_ref[...], b_ref[...], c_ref[...]], axis=1)
    o_ref[...] = x_ref[...] + _mm(mix, w_ref[...], precise)


def out_proj(x, sb_o, fx_o, hg_o, w, *, tm):
    n, d = x.shape
    row = lambda i: (i, 0)
    return pl.pallas_call(
        functools.partial(_out_proj_kernel, precise=w.dtype == F32),
        grid=(n // tm,),
        in_specs=[pl.BlockSpec((tm, d), row),
                  pl.BlockSpec((tm, ATT_W), row),
                  pl.BlockSpec((tm, ATT_W), row),
                  pl.BlockSpec((tm, HG_W), row),
                  pl.BlockSpec(w.shape, lambda i: (0, 0))],
        out_specs=pl.BlockSpec((tm, d), row),
        out_shape=jax.ShapeDtypeStruct((n, d), F32),
        compiler_params=_cparams("parallel"),
        name="out_proj",
    )(x, sb_o, fx_o, hg_o, w)


def _ffn_kernel(h_ref, g_ref, wg_ref, wu_ref, wd_ref, o_ref, hn_ref, acc_ref, *, precise):
    j = pl.program_id(1)

    @pl.when(j == 0)
    def _():
        hn_ref[...] = _rms(h_ref[...], g_ref[...]).astype(hn_ref.dtype)
        acc_ref[...] = jnp.zeros_like(acc_ref)

    hn = hn_ref[...]
    gate = _mm(hn, wg_ref[0], precise)
    up = _mm(hn, wu_ref[0], precise)
    acc_ref[...] += _mm(gate * _sigmoid(gate) * up, wd_ref[0], precise)

    @pl.when(j == pl.num_programs(1) - 1)
    def _():
        o_ref[...] = h_ref[...] + acc_ref[...]


def ffn_dense(h, g_row, wg, wu, wd, layer, *, tm, tf):
    n, d = h.shape
    ff = wg.shape[2]
    precise = wg.dtype == F32
    return pl.pallas_call(
        functools.partial(_ffn_kernel, precise=precise),
        grid=(n // tm, ff // tf),
        in_specs=[pl.BlockSpec((tm, d), lambda i, j: (i, 0)),
                  pl.BlockSpec((1, d), lambda i, j: (0, 0)),
                  pl.BlockSpec((1, d, tf), lambda i, j: (layer, 0, j)),
                  pl.BlockSpec((1, d, tf), lambda i, j: (layer, 0, j)),
                  pl.BlockSpec((1, tf, d), lambda i, j: (layer, j, 0))],
        out_specs=pl.BlockSpec((tm, d), lambda i, j: (i, 0)),
        out_shape=jax.ShapeDtypeStruct((n, d), F32),
        scratch_shapes=[pltpu.VMEM((tm, d), F32 if precise else BF16), pltpu.VMEM((tm, d), F32)],
        compiler_params=_cparams("parallel", "arbitrary"),
        name="ffn_dense",
    )(h, g_row, wg, wu, wd)


def _router_kernel(h_ref, g_ref, w_ref, b_ref, o_ref, *, n_experts):
    hn = _rms(h_ref[...], g_ref[...])
    xh, xl = _split2(hn)
    wh, wl = _split2(w_ref[...])
    logits = _dot(xh, wh) + _dot(xl, wh) + _dot(xh, wl) + b_ref[...]
    lane = lax.broadcasted_iota(jnp.int32, logits.shape, 1).astype(F32)
    logits = jnp.where(lane < n_experts, logits, NEG_BIG)
    m1 = jnp.max(logits, axis=1, keepdims=True)
    i1 = jnp.min(jnp.where(logits == m1, lane, float(LANES)), axis=1, keepdims=True)
    first = lane == i1
    rest = jnp.where(first, NEG_BIG, logits)
    m2 = jnp.max(rest, axis=1, keepdims=True)
    i2 = jnp.min(jnp.where(rest == m2, lane, float(LANES)), axis=1, keepdims=True)
    second = lane == i2
    e = jnp.exp(m2 - m1)
    w1 = 1.0 / (1.0 + e)
    o_ref[...] = jnp.where(first, w1, 0.0) + jnp.where(second, e * w1, 0.0)


def moe_router(h, g_row, w_pad, b_pad, *, tm, n_experts):
    n, d = h.shape
    kern = functools.partial(_router_kernel, n_experts=n_experts)
    return pl.pallas_call(
        kern,
        grid=(n // tm,),
        in_specs=[pl.BlockSpec((tm, d), lambda i: (i, 0)),
                  pl.BlockSpec((1, d), lambda i: (0, 0)),
                  pl.BlockSpec((d, LANES), lambda i: (0, 0)),
                  pl.BlockSpec((1, LANES), lambda i: (0, 0))],
        out_specs=pl.BlockSpec((tm, LANES), lambda i: (i, 0)),
        out_shape=jax.ShapeDtypeStruct((n, LANES), F32),
        compiler_params=_cparams("parallel"),
        name="moe_router",
    )(h, g_row, w_pad, b_pad)


def _moe_kernel(h_ref, g_ref, gates_ref, wg_ref, wu_ref, wd_ref, o_ref, hn_ref, acc_ref, *, precise):
    e = pl.program_id(1)

    @pl.when(e == 0)
    def _():
        hn_ref[...] = _rms(h_ref[...], g_ref[...]).astype(hn_ref.dtype)
        acc_ref[...] = jnp.zeros_like(acc_ref)

    gates = gates_ref[...]
    lane = lax.broadcasted_iota(jnp.int32, gates.shape, 1)
    ge = jnp.sum(jnp.where(lane == e, gates, 0.0), axis=1, keepdims=True)
    hn = hn_ref[...]
    gate = _mm(hn, wg_ref[0, 0], precise)
    up = _mm(hn, wu_ref[0, 0], precise)
    acc_ref[...] += _mm(gate * _sigmoid(gate) * up * ge, wd_ref[0, 0], precise)

    @pl.when(e == pl.num_programs(1) - 1)
    def _():
        o_ref[...] = h_ref[...] + acc_ref[...]


def moe_ffn(h, g_row, gates, wg, wu, wd, layer, *, tm):
    n, d = h.shape
    _, n_e, _, ffe = wg.shape
    precise = wg.dtype == F32
    return pl.pallas_call(
        functools.partial(_moe_kernel, precise=precise),
        grid=(n // tm, n_e),
        in_specs=[pl.BlockSpec((tm, d), lambda i, e: (i, 0)),
                  pl.BlockSpec((1, d), lambda i, e: (0, 0)),
                  pl.BlockSpec((tm, LANES), lambda i, e: (i, 0)),
                  pl.BlockSpec((1, 1, d, ffe), lambda i, e: (layer, e, 0, 0)),
                  pl.BlockSpec((1, 1, d, ffe), lambda i, e: (layer, e, 0, 0)),
                  pl.BlockSpec((1, 1, ffe, d), lambda i, e: (layer, e, 0, 0))],
        out_specs=pl.BlockSpec((tm, d), lambda i, e: (i, 0)),
        out_shape=jax.ShapeDtypeStruct((n, d), F32),
        scratch_shapes=[pltpu.VMEM((tm, d), F32 if precise else BF16), pltpu.VMEM((tm, d), F32)],
        compiler_params=_cparams("parallel", "arbitrary"),
        name="moe_ffn",
    )(h, g_row, gates, wg, wu, wd)


def _head_rows(width):
    r = lax.broadcasted_iota(jnp.int32, (8, width), 0)
    c = lax.broadcasted_iota(jnp.int32, (8, width), 1)
    return (c // HEAD_DIM) == r


def _lane_to_rows(row_vec):
    r = lax.broadcasted_iota(jnp.int32, (8, LANES), 0)
    c = lax.broadcasted_iota(jnp.int32, (8, LANES), 1)
    return jnp.sum(jnp.where(r == c, row_vec, 0.0), axis=1, keepdims=True)


def _finish_decode(acc, own, g_ref, o_ref):
    o = jnp.sum(jnp.where(own, acc, 0.0), axis=0, keepdims=True)
    o_ref[0] = _rms(o, g_ref[...])


def _sb_decode_kernel(pt_ref, q_ref, g_ref, *rest, pps):
    k_refs, v_refs = rest[:pps], rest[pps:2 * pps]
    o_ref, acc_ref, r_ref = rest[2 * pps:]
    c = pl.program_id(1)

    @pl.when(c == 0)
    def _():
        acc_ref[...] = jnp.zeros_like(acc_ref)
        r_ref[...] = jnp.zeros_like(r_ref)

    own = _head_rows(ATT_W)
    qb = jnp.where(own, q_ref[0] * (HEAD_DIM ** -0.5), 0.0)
    u = _tri(LANES, lambda r, cc: r >= cc)
    order = list(reversed(range(pps)))
    zs = [_dot3(qb, k_refs[i][0]) for i in order]
    css = [_dot_ones(_softplus(z), u) for z in zs]
    acc = acc_ref[...]
    run = r_ref[...]
    for i, z, cs in zip(order, zs, css):
        acc = acc + _dot3_nt(jnp.exp(z - cs - run), v_refs[i][0])
        run = run + cs[:, 0:1]
    acc_ref[...] = acc
    r_ref[...] = run

    @pl.when(c == pl.num_programs(1) - 1)
    def _():
        _finish_decode(acc, own, g_ref, o_ref)


def _page_specs(n_chunks, pps, page_off, block):
    def spec(i):
        def index(b, c, pt):
            return (page_off + pt[b, (n_chunks - 1 - c) * pps + i],) + (0,) * (len(block) - 1)
        return pl.BlockSpec(block, index)
    return [spec(i) for i in range(pps)]


def sb_decode(q3, g_row, cache_k, cache_v, page_table, *, page_off, pps):
    nb = q3.shape[0]
    n_pages = page_table.shape[1]
    n_chunks = n_pages // pps
    page = cache_k.shape[2]
    kern = functools.partial(_sb_decode_kernel, pps=pps)
    pages = _page_specs(n_chunks, pps, page_off, (1, ATT_W, page))
    grid_spec = pltpu.PrefetchScalarGridSpec(
        num_scalar_prefetch=1,
        grid=(nb, n_chunks),
        in_specs=[pl.BlockSpec((1, 1, ATT_W), lambda b, c, pt: (b, 0, 0)),
                  pl.BlockSpec((1, ATT_W), lambda b, c, pt: (0, 0))] + pages + pages,
        out_specs=pl.BlockSpec((1, 1, ATT_W), lambda b, c, pt: (b, 0, 0)),
        scratch_shapes=[pltpu.VMEM((8, ATT_W), F32), pltpu.VMEM((8, 1), F32)],
    )
    return pl.pallas_call(
        kern,
        grid_spec=grid_spec,
        out_shape=jax.ShapeDtypeStruct((nb, 1, ATT_W), F32),
        compiler_params=_cparams("parallel", "arbitrary"),
        name="sb_decode",
    )(page_table, q3, g_row, *([cache_k] * pps), *([cache_v] * pps))


def _fox_decode_kernel(pt_ref, q_ref, k_ref, v_ref, f_ref, gq_ref, gk_ref, fb_ref, g_ref, *rest, pps):
    k_refs, v_refs, f_refs = rest[:pps], rest[pps:2 * pps], rest[2 * pps:3 * pps]
    o_ref, kn_ref, lf_ref, acc_ref, m_ref, l_ref, r_ref, qb_ref = rest[3 * pps:]
    c = pl.program_id(1)
    own = _head_rows(ATT_W)

    @pl.when(c == 0)
    def _():
        def head_norm(x_row, gain_row):
            xb = jnp.where(own, x_row, 0.0)
            ms = jnp.sum(xb * xb, axis=1, keepdims=True) * (1.0 / HEAD_DIM)
            return xb * lax.rsqrt(ms + EPS) * gain_row

        qn = head_norm(q_ref[0], gq_ref[...]) * (HEAD_DIM ** -0.5)
        kn = head_norm(k_ref[0], gk_ref[...])
        kn_ref[0] = jnp.sum(kn, axis=0, keepdims=True)
        fx = f_ref[0] + fb_ref[...]
        lf = jnp.minimum(fx, 0.0) - jnp.log(1.0 + jnp.exp(-jnp.abs(fx)))
        lf_ref[0] = lf
        qb_ref[...] = qn
        m_ref[...] = jnp.sum(qn * kn, axis=1, keepdims=True)
        l_ref[...] = jnp.ones_like(l_ref)
        acc_ref[...] = jnp.broadcast_to(v_ref[0], acc_ref.shape)
        r_ref[...] = _lane_to_rows(lf)

    qb = qb_ref[...]
    u = _tri(LANES, lambda r, cc: r >= cc)
    m_old, run = m_ref[...], r_ref[...]
    pad = jnp.zeros((8 - N_HEADS, LANES), F32)
    order = list(reversed(range(pps)))
    zs = [_dot3(qb, k_refs[i][0]) for i in order]
    lfs =[jnp.concatenate([f_refs[i][0], pad], axis=0) for i in order]
    css = [_dot_ones(lf, u) for lf in lfs]
    ss = []
    for z, lf, cs in zip(zs, lfs, css):
        ss.append(z + (cs - lf) + run)
        run = run + cs[:, 0:1]
    m_new = m_old
    for s in ss:
        m_new = jnp.maximum(m_new, jnp.max(s, axis=1, keepdims=True))
    alpha = jnp.exp(m_old - m_new)
    l = alpha * l_ref[...]
    acc = alpha * acc_ref[...]
    for i, s in zip(order, ss):
        pr = jnp.exp(s - m_new)
        l = l + jnp.sum(pr, axis=1, keepdims=True)
        acc = acc + _dot3_nt(pr, v_refs[i][0])
    acc_ref[...], m_ref[...], l_ref[...], r_ref[...] = acc, m_new, l, run

    @pl.when(c == pl.num_programs(1) - 1)
    def _():
        _finish_decode(acc / l, own, g_ref, o_ref)


def fox_decode(q3, k3, v3, f3, gq_row, gk_row, fb_row, g_row, cache_k, cache_v, cache_ft, page_table,
               *, page_off, pps):
    nb = q3.shape[0]
    n_pages = page_table.shape[1]
    n_chunks = n_pages // pps
    page = cache_k.shape[2]
    kern = functools.partial(_fox_decode_kernel, pps=pps)
    pages = _page_specs(n_chunks, pps, page_off, (1, ATT_W, page))
    f_pages = _page_specs(n_chunks, pps, page_off, (1, N_HEADS, page))
    tok = lambda w: pl.BlockSpec((1, 1, w), lambda b, c, pt: (b, 0, 0))
    par = lambda w: pl.BlockSpec((1, w), lambda b, c, pt: (0, 0))
    grid_spec = pltpu.PrefetchScalarGridSpec(
        num_scalar_prefetch=1,
        grid=(nb, n_chunks),
        in_specs=[tok(ATT_W), tok(ATT_W), tok(ATT_W), tok(LANES),
                  par(ATT_W), par(ATT_W), par(LANES), par(ATT_W)] + pages + pages + f_pages,
        out_specs=[tok(ATT_W), tok(ATT_W), tok(LANES)],
        scratch_shapes=[pltpu.VMEM((8, ATT_W), F32), pltpu.VMEM((8, 1), F32), pltpu.VMEM((8, 1), F32),
                        pltpu.VMEM((8, 1), F32), pltpu.VMEM((8, ATT_W), F32)],
    )
    return pl.pallas_call(
        kern,
        grid_spec=grid_spec,
        out_shape=[jax.ShapeDtypeStruct((nb, 1, ATT_W), F32),
                   jax.ShapeDtypeStruct((nb, 1, ATT_W), F32),
                   jax.ShapeDtypeStruct((nb, 1, LANES), F32)],
        compiler_params=_cparams("parallel", "arbitrary"),
        name="fox_decode",
    )(page_table, q3, k3, v3, f3, gq_row, gk_row, fb_row, g_row,
      *([cache_k] * pps), *([cache_v] * pps), *([cache_ft] * pps))


def _row_to_col(row_vec):
    r = lax.broadcasted_iota(jnp.int32, (HG_D, HG_D), 0)
    c = lax.broadcasted_iota(jnp.int32, (HG_D, HG_D), 1)
    return jnp.sum(jnp.where(r == c, row_vec, 0.0), axis=1, keepdims=True)


def _hgrn_decode_kernel(hq_ref, hf_ref, hi_ref, hg_ref, lb_ref, g_ref, s_ref, o_ref, sn_ref):
    outs = []
    for h in range(N_HEADS):
        ws = slice(h * HG_D, (h + 1) * HG_D)
        q, f, k = _hgrn_gates(hq_ref[0][:, ws], hf_ref[0][:, ws], lb_ref[:, ws])
        s_new = s_ref[0, h] * _row_to_col(f) + _row_to_col(k) * hi_ref[0][:, ws]
        sn_ref[0, h] = s_new
        o = jnp.sum(_row_to_col(q) * s_new, axis=0, keepdims=True)
        outs.append(_rms(o, g_ref[...]) * _sigmoid(hg_ref[0][:, ws]))
    o_ref[0] = jnp.concatenate(outs, axis=1)


def hgrn_decode(hq3, hf3, hi3, hg3, lb_row, g_row, state):
    nb = hq3.shape[0]
    tok = pl.BlockSpec((1, 1, HG_W), lambda b: (b, 0, 0))
    st = pl.BlockSpec((1, N_HEADS, HG_D, HG_D), lambda b: (b, 0, 0, 0))
    return pl.pallas_call(
        _hgrn_decode_kernel,
        grid=(nb,),
        in_specs=[tok, tok, tok, tok,
                  pl.BlockSpec((1, HG_W), lambda b: (0, 0)),
                  pl.BlockSpec((1, HG_D), lambda b: (0, 0)), st],
        out_specs=[tok, st],
        out_shape=[jax.ShapeDtypeStruct((nb, 1, HG_W), F32),
                   jax.ShapeDtypeStruct(state.shape, F32)],
        compiler_params=_cparams("parallel"),
        name="hgrn_decode",
    )(hq3, hf3, hi3, hg3, lb_row, g_row, state)


def _arrange_w_in(w_in_l):
    d = w_in_l.shape[0]
    a = 2 * 3 * ATT_W
    main = jnp.concatenate([w_in_l[:, :a], w_in_l[:, a + N_HEADS:]], axis=1)
    fcols = w_in_l[:, a:a + N_HEADS]
    pad = jnp.zeros((d, N_IN_PAD - C_FXF - N_HEADS), w_in_l.dtype)
    return jnp.concatenate([main, fcols, pad], axis=1).astype(F32)


def _seg_matrix():
    r = lax.broadcasted_iota(jnp.int32, (ATT_W, ATT_W), 0) // HEAD_DIM
    c = lax.broadcasted_iota(jnp.int32, (ATT_W, ATT_W), 1) // HEAD_DIM
    return jnp.where(r == c, 1.0, 0.0).astype(BF16)


def _row(v, width=None):
    v = v.reshape(1, -1).astype(F32)
    if width is not None and v.shape[1] < width:
        v = jnp.pad(v, ((0, 0), (0, width - v.shape[1])))
    return v


def kernel(x_prompt, x_sample, cache_sb_k, cache_sb_v, cache_fox_k, cache_fox_v, cache_fox_logf, state_hgrn,
           page_table, w_in, w_out, norm_mix_g, norm_ffn_g, fox_q_norm_g, fox_k_norm_g, fox_f_bias, sb_out_g,
           fox_out_g, hgrn_out_g, hgrn_lb_logits, ffn_w_gate, ffn_w_up, ffn_w_down, moe_router_w,
           moe_router_b, moe_w_gate, moe_w_up, moe_w_down):
    batch, seq, d_model = x_prompt.shape
    dec_b = x_sample.shape[0]
    depth = w_in.shape[0]
    n_pool, page = cache_sb_k.shape[1], cache_sb_k.shape[2]
    n_experts = moe_router_w.shape[-1]

    lb_all = lower_bounds(hgrn_lb_logits.astype(F32))
    seg = _seg_matrix()
    def pages_t(c):
        return jnp.transpose(c, (0, 1, 3, 4, 2)).reshape(depth * n_pool, ATT_W, page)

    sbk_c, sbv_c, fxk_c, fxv_c = (pages_t(c) for c in (cache_sb_k, cache_sb_v, cache_fox_k, cache_fox_v))
    fxf_c = jnp.swapaxes(cache_fox_logf, 2, 3).reshape(depth * n_pool, N_HEADS, page)

    xp = x_prompt.reshape(batch * seq, d_model)
    xs = x_sample.reshape(dec_b, d_model)
    p_out = [[] for _ in range(6)]
    s_out = [[] for _ in range(6)]
    kv_stacks, fox_stacks = None, None

    for l in range(depth):
        w_in_f = _arrange_w_in(w_in[l])
        w_in_l = w_in_f.astype(BF16)
        w_out_f = w_out[l].astype(F32)
        w_out_l = w_out_f.astype(BF16)
        g_mix, g_ffn = _row(norm_mix_g[l]), _row(norm_ffn_g[l])
        gq = _row(jnp.tile(fox_q_norm_g[l], N_HEADS))
        gk = _row(jnp.tile(fox_k_norm_g[l], N_HEADS))
        fb = _row(fox_f_bias[l], LANES)
        g_sb, g_fx, g_hg = _row(sb_out_g[l]), _row(fox_out_g[l]), _row(hgrn_out_g[l])
        lb_row = lb_all[l:l + 1]

        p, kv_stacks = in_proj_stacked(xp, g_mix, w_in_l, kv_stacks, l, depth, batch, tm=512, tn=1280)
        kn_stack, lf_stack, qa, ka = fox_prep(p, gq, gk, fb, seg, fox_stacks, l, depth,
                                              batch=batch, seq=seq, tb=256)
        fox_stacks = (kn_stack, lf_stack)
        sb_o = sb_attention(p, g_sb, batch=batch, seq=seq, tq=512, tk=256)
        fx_o = fox_attention(qa, ka, p, g_fx, batch=batch, seq=seq, tq=256)
        hg_o, s_fin = hgrn_prompt(p, lb_row, g_hg, batch=batch, seq=seq, chunk=128, sub=8)
        hp = out_proj(xp, sb_o, fx_o, hg_o, w_out_l, tm=512)
        p_out[5].append(s_fin)

        ps = in_proj(xs, g_mix, w_in_f, tm=dec_b, tn=1280)
        tok = lambda c0, w: ps[:, c0:c0 + w].reshape(dec_b, 1, w)
        off = l * n_pool
        sb_os = sb_decode(tok(C_SBQ, ATT_W), g_sb, sbk_c, sbv_c, page_table, page_off=off, pps=32)
        fx_os, fxk_s, lf_s = fox_decode(tok(C_FXQ, ATT_W), tok(C_FXK, ATT_W), tok(C_FXV, ATT_W),
                                        tok(C_FXF, LANES), gq, gk, fb, g_fx, fxk_c, fxv_c, fxf_c,
                                        page_table, page_off=off, pps=32)
        hg_os, s_new = hgrn_decode(tok(C_HGQ, HG_W), tok(C_HGF, HG_W), tok(C_HGI, HG_W), tok(C_HGG, HG_W),
                                   lb_row, g_hg, state_hgrn[l].astype(F32))
        hs = out_proj(xs, sb_os.reshape(dec_b, ATT_W), fx_os.reshape(dec_b, ATT_W),
                      hg_os.reshape(dec_b, HG_W), w_out_f, tm=dec_b)
        s_out[0].append(ps[:, C_SBK:C_SBK + ATT_W])
        s_out[1].append(ps[:, C_SBV:C_SBV + ATT_W])
        s_out[2].append(fxk_s.reshape(dec_b, ATT_W))
        s_out[3].append(ps[:, C_FXV:C_FXV + ATT_W])
        s_out[4].append(lf_s.reshape(dec_b, LANES)[:, :N_HEADS])
        s_out[5].append(s_new)

        if l % 2 == 0:
            wf = [w.astype(F32) for w in (ffn_w_gate, ffn_w_up, ffn_w_down)]
            wg, wu, wd = (w[l // 2][None].astype(BF16) for w in wf)
            xp = ffn_dense(hp, g_ffn, wg, wu, wd, 0, tm=1024, tf=256)
            xs = ffn_dense(hs, g_ffn, *wf, l // 2, tm=dec_b, tf=256)
        else:
            wf = [w.astype(F32) for w in (moe_w_gate, moe_w_up, moe_w_down)]
            wg, wu, wd = (w[l // 2][None].astype(BF16) for w in wf)
            rw = jnp.pad(moe_router_w[l // 2].astype(F32), ((0, 0), (0, LANES - n_experts)))
            rb = _row(moe_router_b[l // 2], LANES)
            gates_p = moe_router(hp, g_ffn, rw, rb, tm=512, n_experts=n_experts)
            xp = moe_ffn(hp, g_ffn, gates_p, wg, wu, wd, 0, tm=512)
            gates_s = moe_router(hs, g_ffn, rw, rb, tm=dec_b, n_experts=n_experts)
            xs = moe_ffn(hs, g_ffn, gates_s, *wf, l // 2, tm=dec_b)

    def stack(parts, shape):
        return jnp.stack(parts).reshape((depth,) + shape)

    hd = (N_HEADS, HEAD_DIM)
    y_prompt = xp.reshape(batch, seq, d_model)
    y_sample = xs.reshape(dec_b, 1, d_model)
    sbk_p, sbv_p, fxv_p = kv_stacks
    kn_stack, lf_stack = fox_stacks
    outs_p = [jnp.transpose(a.reshape((depth, batch) + hd + (seq,)), (0, 1, 4, 2, 3))
              for a in (sbk_p, sbv_p, kn_stack, fxv_p)]
    outs_p.append(jnp.swapaxes(lf_stack, 2, 3))
    outs_p.append(stack(p_out[5], (batch, N_HEADS, HG_D, HG_D)).astype(state_hgrn.dtype))
    outs_s = [stack(s_out[i], (dec_b, 1) + hd) for i in range(4)]
    outs_s.append(stack(s_out[4], (dec_b, 1, N_HEADS)))
    outs_s.append(stack(s_out[5], (dec_b, N_HEADS, HG_D, HG_D)).astype(state_hgrn.dtype))
    return (y_prompt, y_sample, *outs_p, *outs_s)
```
